```python
import jax, jax.numpy as jnp
from jax import lax
import numpy as np

D_MODEL = 1024
BATCH = 8
SEQ = 4096
DEPTH = 1
DEC_BATCH = 128
DEC_SEQ = 1
PAST_LEN = 8192
PAGE_SIZE = 128

MERGE_W = D_MODEL
ATT_HEAD_DIM = 64
ATT_HEADS = MERGE_W // ATT_HEAD_DIM
ATT_W = ATT_HEADS * ATT_HEAD_DIM
BLOCK = 256
TOP_K_BLOCKS = 3
Q_CHUNK = 16
ATT_SCALE = ATT_HEAD_DIM ** -0.5
RWKV_HEAD_DIM = 64
RWKV_HEADS = MERGE_W // RWKV_HEAD_DIM
RWKV_W = RWKV_HEADS * RWKV_HEAD_DIM
DECAY_LORA = 64
AAA_LORA = 64
GATE_LORA = 128
SHIFT_W = 3 * RWKV_W + DECAY_LORA + AAA_LORA + GATE_LORA
OFF_RWKV = 3 * ATT_W
OFF_GATE = OFF_RWKV + SHIFT_W
PROJ_W = OFF_GATE + ATT_W + RWKV_W
D_FF = 4 * D_MODEL
NORM_EPS = 1e-6
GN_EPS = 6.4e-4

kernel_name = 'moba_rwkv7_parallel_hybrid_step'

F32 = jnp.float32


def _rmsnorm(x, g):
    xf = x.astype(F32)
    y = xf * lax.rsqrt(jnp.mean(xf * xf, axis=-1, keepdims=True) + NORM_EPS) * g.astype(F32)
    return y.astype(x.dtype)


def _project(x, g_mix, w_in):
    return _rmsnorm(x, g_mix) @ w_in


def _attn_qkv(p, g_q, g_k):
    B, T, _ = p.shape
    hs = (B, T, ATT_HEADS, ATT_HEAD_DIM)
    q = _rmsnorm(p[..., :ATT_W].reshape(hs), g_q)
    k = _rmsnorm(p[..., ATT_W:2 * ATT_W].reshape(hs), g_k)
    v = p[..., 2 * ATT_W:3 * ATT_W].reshape(hs)
    return q, k, v


def _moba_core(q, q_pos, k_mean, fetch):
    T = q.shape[1]
    nb = k_mean.shape[1]
    qb = q_pos // BLOCK
    gate = jnp.einsum('bthd,bnhd->bhtn', q.astype(F32), k_mean)
    gate = jnp.where(jnp.arange(nb)[None, :] < qb[:, None], gate, -jnp.inf)
    n_sel = min(TOP_K_BLOCKS, nb)
    _, sel = lax.top_k(gate, n_sel)
    own = jnp.broadcast_to(qb[None, None, :, None], sel.shape[:3] + (1,)).astype(sel.dtype)
    idx = jnp.concatenate([sel, own], axis=-1)
    slot_ok = jnp.concatenate([jnp.arange(n_sel)[None, :] < qb[:, None],
                               jnp.ones((T, 1), dtype=bool)], axis=-1)
    k_sel, v_sel = fetch(idx)
    key_pos = idx[..., None] * BLOCK + jnp.arange(BLOCK)
    mask = slot_ok[None, None, :, :, None] & (key_pos <= q_pos[None, None, :, None, None])
    s = jnp.einsum('bthd,bhtskd->bhtsk', q, k_sel, preferred_element_type=F32) * ATT_SCALE
    p = jax.nn.softmax(jnp.where(mask, s, -jnp.inf), axis=(-2, -1))
    o = jnp.einsum('bhtsk,bhtskd->bthd', p.astype(v_sel.dtype), v_sel, preferred_element_type=F32)
    return o.astype(q.dtype)


def _moba_prompt(q, k, v):
    B, T, H, Dh = q.shape
    nb = -(-T // BLOCK)
    pad = ((0, 0), (0, nb * BLOCK - T), (0, 0), (0, 0))
    k_blk = jnp.pad(k, pad).reshape(B, nb, BLOCK, H, Dh)
    v_blk = jnp.pad(v, pad).reshape(B, nb, BLOCK, H, Dh)
    k_mean = jnp.mean(k_blk.astype(F32), axis=2)
    bi = jnp.arange(B)[:, None, None, None]
    hi = jnp.arange(H)[None, :, None, None]

    def fetch(idx):
        return k_blk[bi, idx, :, hi], v_blk[bi, idx, :, hi]

    nc = T // Q_CHUNK
    q_c = jnp.moveaxis(q.reshape(B, nc, Q_CHUNK, H, Dh), 1, 0)
    pos_c = jnp.arange(T, dtype=jnp.int32).reshape(nc, Q_CHUNK)

    def body(args):
        qc, pc = args
        return _moba_core(qc, pc, k_mean, fetch)

    out = lax.map(body, (q_c, pos_c))
    return jnp.moveaxis(out, 0, 1).reshape(B, T, H * Dh)


def _moba_sample(q, k_new, v_new, pool_k, pool_v, page_table):
    B, T, H, Dh = q.shape
    n_pages = page_table.shape[1]
    past = n_pages * PAGE_SIZE
    nb = -(-(past + T) // BLOCK)
    page_sum = jnp.sum(pool_k, axis=1, dtype=F32)[page_table]
    onehot_p = jax.nn.one_hot((jnp.arange(n_pages) * PAGE_SIZE) // BLOCK, nb, dtype=F32)
    onehot_n = jax.nn.one_hot((past + jnp.arange(T)) // BLOCK, nb, dtype=F32)
    k_mean = (jnp.einsum('bphd,pn->bnhd', page_sum, onehot_p)
              + jnp.einsum('bthd,tn->bnhd', k_new.astype(F32), onehot_n)) / BLOCK
    bi = jnp.arange(B)[:, None, None, None, None]
    hi = jnp.arange(H)[None, :, None, None, None]

    def fetch(idx):
        pos = idx[..., None] * BLOCK + jnp.arange(BLOCK)
        in_past = (pos < past)[..., None]
        phys = page_table[bi, jnp.minimum(pos // PAGE_SIZE, n_pages - 1)]
        row = pos % PAGE_SIZE
        new_row = jnp.clip(pos - past, 0, T - 1)
        k_sel = jnp.where(in_past, pool_k[phys, row, hi], k_new[bi, new_row, hi])
        v_sel = jnp.where(in_past, pool_v[phys, row, hi], v_new[bi, new_row, hi])
        return k_sel, v_sel

    q_pos = past + jnp.arange(T, dtype=jnp.int32)
    return _moba_core(q, q_pos, k_mean, fetch).reshape(B, T, H * Dh)


def _wkv_scan(s0, r, w, k, v, kk, a):
    def step(S, inp):
        r_t, w_t, k_t, v_t, kk_t, a_t = inp
        sa = jnp.einsum('bhij,bhj->bhi', S, -kk_t)
        S = (S * w_t[:, :, None, :] + sa[..., None] * (kk_t * a_t)[:, :, None, :]
             + v_t[..., None] * k_t[:, :, None, :])
        return S, jnp.einsum('bhij,bhj->bhi', S, r_t)

    xs = tuple(jnp.moveaxis(t, 1, 0) for t in (r, w, k, v, kk, a))
    S, ys = lax.scan(step, s0, xs)
    return jnp.moveaxis(ys, 0, 1), S


def _rwkv_branch(u_in, shift0, wkv0, mu, w0, w_w2, a0, w_a2, w_g2, k_k, k_a, r_k, ln_w, ln_b):
    B, T, _ = u_in.shape
    prev = jnp.concatenate([shift0[:, None, :].astype(u_in.dtype), u_in[:, :-1]], axis=1)
    u = u_in + (prev - u_in) * mu
    c1, c2, c3 = RWKV_W, 2 * RWKV_W, 3 * RWKV_W
    r, k, v = u[..., :c1], u[..., c1:c2], u[..., c2:c3]
    w_lo = u[..., c3:c3 + DECAY_LORA]
    a_lo = u[..., c3 + DECAY_LORA:c3 + DECAY_LORA + AAA_LORA]
    g_lo = u[..., c3 + DECAY_LORA + AAA_LORA:]
    w_log = -jax.nn.softplus(-(w0 + jnp.tanh(w_lo) @ w_w2).astype(F32)) - 0.5
    decay = jnp.exp(-jnp.exp(w_log))
    a = jax.nn.sigmoid((a0 + a_lo @ w_a2).astype(F32))
    g = jax.nn.sigmoid(g_lo) @ w_g2
    hs = (B, T, RWKV_HEADS, RWKV_HEAD_DIM)
    hd = (RWKV_HEADS, RWKV_HEAD_DIM)
    r, k, v = r.astype(F32).reshape(hs), k.astype(F32).reshape(hs), v.astype(F32).reshape(hs)
    decay, a = decay.reshape(hs), a.reshape(hs)
    kk = k * k_k.astype(F32).reshape(hd)
    kk = kk * lax.rsqrt(jnp.maximum(jnp.sum(kk * kk, axis=-1, keepdims=True), 1e-24))
    k = k * (1.0 + (a - 1.0) * k_a.astype(F32).reshape(hd))
    y, wkv = _wkv_scan(wkv0.astype(F32), r, decay, k, v, kk, a)
    mean = jnp.mean(y, axis=-1, keepdims=True)
    var = jnp.mean(jnp.square(y - mean), axis=-1, keepdims=True)
    y = ((y - mean) * lax.rsqrt(var + GN_EPS)).reshape(B, T, RWKV_W) * ln_w.astype(F32) + ln_b.astype(F32)
    bonus = (jnp.sum(r * k * r_k.astype(F32), axis=-1, keepdims=True) * v).reshape(B, T, RWKV_W)
    o = (y + bonus) * g.astype(F32)
    return o.astype(u_in.dtype), wkv.astype(wkv0.dtype), u_in[:, -1]


def _merge_and_ffn(x, gates, o_a, o_b, w_o, g_ffn, w_up, w_down):
    gate_a = jax.nn.sigmoid(gates[..., :ATT_W])
    gate_b = jax.nn.sigmoid(gates[..., ATT_W:])
    x = x + (gate_a * o_a + gate_b * o_b) @ w_o
    h = _rmsnorm(x, g_ffn)
    return x + jnp.square(jax.nn.relu(h @ w_up)) @ w_down


def setup_inputs(seed: int = 0) -> dict:
    key = jax.random.key(seed)
    ks = jax.random.split(key, 28)
    n_pages = PAST_LEN // PAGE_SIZE
    n_used = DEC_BATCH * n_pages
    n_phys = n_used + max(1, n_used // 4)

    def nrm(k, shape, scale):
        return scale * jax.random.normal(k, shape, F32)

    kv_shape = (DEPTH, n_phys, PAGE_SIZE, ATT_HEADS, ATT_HEAD_DIM)
    page_table = jax.random.permutation(ks[0], n_phys)[:n_used].reshape(DEC_BATCH, n_pages).astype(jnp.int32)
    decay_base = jnp.linspace(-6.5, -1.5, RWKV_W, dtype=F32)
    return {
        'x_prompt': nrm(ks[1], (BATCH, SEQ, D_MODEL), 1.0),
        'x_sample': nrm(ks[2], (DEC_BATCH, DEC_SEQ, D_MODEL), 1.0),
        'cache_k': nrm(ks[3], kv_shape, 1.0),
        'cache_v': nrm(ks[4], kv_shape, 1.0),
        'state_wkv': nrm(ks[5], (DEPTH, DEC_BATCH, RWKV_HEADS, RWKV_HEAD_DIM, RWKV_HEAD_DIM), 0.5),
        'state_shift': nrm(ks[6], (DEPTH, DEC_BATCH, SHIFT_W), 1.0),
        'page_table': page_table,
        'g_mix': 1.0 + nrm(ks[7], (DEPTH, D_MODEL), 0.05),
        'w_in': nrm(ks[8], (DEPTH, D_MODEL, PROJ_W), D_MODEL ** -0.5),
        'g_q': 1.0 + nrm(ks[9], (DEPTH, ATT_HEAD_DIM), 0.05),
        'g_k': 1.0 + nrm(ks[10], (DEPTH, ATT_HEAD_DIM), 0.05),
        'mu_shift': jax.random.uniform(ks[11], (DEPTH, SHIFT_W), F32, 0.05, 0.95),
        'w0': decay_base + nrm(ks[12], (DEPTH, RWKV_W), 0.05),
        'w_w2': nrm(ks[13], (DEPTH, DECAY_LORA, RWKV_W), 0.1 * DECAY_LORA ** -0.5),
        'a0': nrm(ks[14], (DEPTH, RWKV_W), 0.1),
        'w_a2': nrm(ks[15], (DEPTH, AAA_LORA, RWKV_W), 0.5 * AAA_LORA ** -0.5),
        'w_g2': nrm(ks[16], (DEPTH, GATE_LORA, RWKV_W), GATE_LORA ** -0.5),
        'k_k': 0.85 + nrm(ks[17], (DEPTH, RWKV_W), 0.05),
        'k_a': 1.0 + nrm(ks[18], (DEPTH, RWKV_W), 0.05),
        'r_k': nrm(ks[19], (DEPTH, RWKV_HEADS, RWKV_HEAD_DIM), 0.1),
        'ln_x_w': 1.0 + nrm(ks[20], (DEPTH, RWKV_W), 0.05),
        'ln_x_b': nrm(ks[21], (DEPTH, RWKV_W), 0.02),
        'w_o': nrm(ks[22], (DEPTH, MERGE_W, D_MODEL), MERGE_W ** -0.5),
        'g_ffn': 1.0 + nrm(ks[23], (DEPTH, D_MODEL), 0.05),
        'w_up': nrm(ks[24], (DEPTH, D_MODEL, D_FF), D_MODEL ** -0.5),
        'w_down': nrm(ks[25], (DEPTH, D_FF, D_MODEL), D_FF ** -0.5),
    }


def reference(x_prompt, x_sample, cache_k, cache_v, state_wkv, state_shift, page_table,
              g_mix, w_in, g_q, g_k, mu_shift, w0, w_w2, a0, w_a2, w_g2, k_k, k_a, r_k,
              ln_x_w, ln_x_b, w_o, g_ffn, w_up, w_down):
    xp, xs = x_prompt, x_sample
    kp_l, vp_l, ks_l, vs_l, hp_l, hs_l, sp_l, ss_l = [], [], [], [], [], [], [], []
    for l in range(DEPTH):
        rw = (mu_shift[l], w0[l], w_w2[l], a0[l], w_a2[l], w_g2[l], k_k[l], k_a[l], r_k[l],
              ln_x_w[l], ln_x_b[l])
        pp = _project(xp, g_mix[l], w_in[l])
        qp, kp, vp = _attn_qkv(pp, g_q[l], g_k[l])
        oa_p = _moba_prompt(qp, kp, vp)
        sh0 = jnp.zeros((xp.shape[0], SHIFT_W), pp.dtype)
        wkv0 = jnp.zeros((xp.shape[0], RWKV_HEADS, RWKV_HEAD_DIM, RWKV_HEAD_DIM), state_wkv.dtype)
        ob_p, wkv_p, sh_p = _rwkv_branch(pp[..., OFF_RWKV:OFF_GATE], sh0, wkv0, *rw)
        xp = _merge_and_ffn(xp, pp[..., OFF_GATE:], oa_p, ob_p, w_o[l], g_ffn[l], w_up[l], w_down[l])
        ps = _project(xs, g_mix[l], w_in[l])
        qs, kn, vn = _attn_qkv(ps, g_q[l], g_k[l])
        oa_s = _moba_sample(qs, kn, vn, cache_k[l], cache_v[l], page_table)
        ob_s, wkv_s, sh_s = _rwkv_branch(ps[..., OFF_RWKV:OFF_GATE], state_shift[l], state_wkv[l], *rw)
        xs = _merge_and_ffn(xs, ps[..., OFF_GATE:], oa_s, ob_s, w_o[l], g_ffn[l], w_up[l], w_down[l])
        kp_l.append(kp); vp_l.append(vp); ks_l.append(kn); vs_l.append(vn)
        hp_l.append(wkv_p); hs_l.append(wkv_s); sp_l.append(sh_p); ss_l.append(sh_s)
    k_prompt, v_prompt = jnp.stack(kp_l), jnp.stack(vp_l)
    k_sample, v_sample = jnp.stack(ks_l), jnp.stack(vs_l)
    wkv_prompt, wkv_sample = jnp.stack(hp_l), jnp.stack(hs_l)
    shift_prompt, shift_sample = jnp.stack(sp_l), jnp.stack(ss_l)
    return (xp, xs, k_prompt, v_prompt, k_sample, v_sample, wkv_prompt, wkv_sample, shift_prompt, shift_sample)
```

```python
import functools

import jax
import jax.numpy as jnp
from jax import lax
from jax.experimental import pallas as pl
from jax.experimental.pallas import tpu as pltpu

F32 = jnp.float32
BF16 = jnp.bfloat16
I32 = jnp.int32

D_MODEL = 1024
HEADS = 16
HEAD_DIM = 64
MOBA_BLOCK = 256
TOP_K = 3
PAGE_ROWS = 128
PAGES_PER_BLOCK = MOBA_BLOCK // PAGE_ROWS
ATT_SCALE = HEAD_DIM ** -0.5
DECAY_LORA = 64
AAA_LORA = 64
GATE_LORA = 128
LORA_W = DECAY_LORA + AAA_LORA + GATE_LORA
D_FF = 4 * D_MODEL
NORM_EPS = 1e-6
GN_EPS = 6.4e-4
EXP_M05 = 0.6065306597126334
MASK_NEG = -1e30

COL_Q, COL_K, COL_V, COL_R, COL_RK, COL_RV, COL_GA, COL_GB = range(8)
COL_LORA = 8 * D_MODEL // LORA_W
PROJ_W = 8 * D_MODEL + LORA_W

MXU_LANES = 256
VREG_LANES = 128
HEADS_PER_GROUP = MXU_LANES // HEAD_DIM
N_GROUPS = D_MODEL // MXU_LANES
HEAD_PAIRS = D_MODEL // VREG_LANES
CHUNK = 64
assert CHUNK == HEAD_DIM
VMEM_LIMIT_CAP = 56 * 1024 * 1024


def _cparams(n_axes, vmem_mib):
    return pltpu.CompilerParams(
        dimension_semantics=("arbitrary",) * n_axes,
        vmem_limit_bytes=min(vmem_mib * 1024 * 1024, VMEM_LIMIT_CAP),
    )


def _dot(a, b):
    return jnp.dot(a, b, preferred_element_type=F32)


def _dot_nt(a, b):
    return lax.dot_general(a, b, (((1,), (1,)), ((), ())), preferred_element_type=F32)


def _dot_tn(a, b):
    return lax.dot_general(a, b, (((0,), (0,)), ((), ())), preferred_element_type=F32)


def _split2(x):
    hi = x.astype(BF16)
    lo = (x - hi.astype(F32)).astype(BF16)
    return hi, lo


def _split3(x):
    h1 = x.astype(BF16)
    r1 = x - h1.astype(F32)
    h2 = r1.astype(BF16)
    h3 = (r1 - h2.astype(F32)).astype(BF16)
    return h1, h2, h3


def _dot_f32(a, b):
    a_hi, a_lo = _split2(a)
    b_hi, b_lo = _split2(b)
    return _dot(a_hi, b_hi) + _dot(a_hi, b_lo) + _dot(a_lo, b_hi)


def _iota(shape, axis):
    return lax.broadcasted_iota(I32, shape, axis)


def _head_ones():
    r = lax.shift_right_logical(_iota((MXU_LANES, MXU_LANES), 0), 6)
    c = lax.shift_right_logical(_iota((MXU_LANES, MXU_LANES), 1), 6)
    return (r == c).astype(BF16)


def _head_sum(x, head_ones):
    hi, lo = _split2(x)
    parts = []
    for g in range(N_GROUPS):
        sl = slice(g * MXU_LANES, (g + 1) * MXU_LANES)
        parts.append(_dot(hi[:, sl], head_ones) + _dot(lo[:, sl], head_ones))
    return jnp.concatenate(parts, axis=1)


def _rmsnorm(x, g):
    ms = jnp.mean(x * x, axis=-1, keepdims=True)
    return x * lax.rsqrt(ms + NORM_EPS) * g


def _head_rmsnorm(x, g, head_ones):
    ss = _head_sum(x * x, head_ones)
    return x * lax.rsqrt(ss * (1.0 / HEAD_DIM) + NORM_EPS) * g


def _rwkv_mix(ur, uk, ul, w0, a0, w_lora, k_k, k_a, head_ones):
    lane = _iota((1, LORA_W), 1)
    t = jnp.where(lane < DECAY_LORA, jnp.tanh(ul),
                  jnp.where(lane < DECAY_LORA + AAA_LORA, ul, jax.nn.sigmoid(ul)))
    lo = _dot(t.astype(BF16), w_lora)
    e = EXP_M05 * jax.nn.sigmoid(w0 + lo[:, :D_MODEL])
    a = jax.nn.sigmoid(a0 + lo[:, D_MODEL:2 * D_MODEL])
    g = lo[:, 2 * D_MODEL:]
    kk = uk * k_k
    kk = kk * lax.rsqrt(jnp.maximum(_head_sum(kk * kk, head_ones), 1e-24))
    kmod = uk * (1.0 + (a - 1.0) * k_a)
    return e, a, g, kk, kmod


def _rwkv_out(y, ur, kmod, uv, g, r_k, ln_w, ln_b, head_ones):
    inv = 1.0 / HEAD_DIM
    mean = _head_sum(y, head_ones) * inv
    d = y - mean
    var = _head_sum(d * d, head_ones) * inv
    yn = d * lax.rsqrt(var + GN_EPS) * ln_w + ln_b
    bonus = _head_sum(ur * kmod * r_k, head_ones) * uv
    return (yn + bonus) * g


def _proj_kernel(x_ref, g_ref, w_ref, o_ref, h_ref):
    @pl.when(pl.program_id(1) == 0)
    def _():
        h_ref[...] = _rmsnorm(x_ref[...], g_ref[...]).astype(BF16)

    o_ref[...] = _dot(h_ref[...], w_ref[...])


def _project(x, g_mix, w_perm):
    m = x.shape[0]
    tm = min(1024, m)
    tn = PROJ_W // 6
    return pl.pallas_call(
        _proj_kernel,
        out_shape=jax.ShapeDtypeStruct((m, PROJ_W), F32),
        grid=(m // tm, PROJ_W // tn),
        in_specs=[
            pl.BlockSpec((tm, D_MODEL), lambda i, j: (i, 0)),
            pl.BlockSpec((1, D_MODEL), lambda i, j: (0, 0)),
            pl.BlockSpec((D_MODEL, tn), lambda i, j: (0, j)),
        ],
        out_specs=pl.BlockSpec((tm, tn), lambda i, j: (i, j)),
        scratch_shapes=[pltpu.VMEM((tm, D_MODEL), BF16)],
        compiler_params=_cparams(2, 40),
        name="proj",
    )(x, g_mix, w_perm)


def _attn_prep_kernel(q_ref, k_ref, v_ref, gq_ref, gk_ref,
                      kout_ref, vout_ref, qp_ref, kp_ref, vt_ref, kmean_ref):
    head_ones = _head_ones()
    qn = _head_rmsnorm(q_ref[...], gq_ref[...], head_ones) * ATT_SCALE
    kn = _head_rmsnorm(k_ref[...], gk_ref[...], head_ones)
    v = v_ref[...]
    kout_ref[...] = kn
    vout_ref[...] = v
    for p in range(HEAD_PAIRS):
        sl = slice(p * VREG_LANES, (p + 1) * VREG_LANES)
        qp_ref[0, p] = qn[:, sl].astype(BF16)
        kp_ref[0, p] = kn[:, sl].astype(BF16)
    vt = v.T.astype(BF16)
    for p in range(HEAD_PAIRS):
        vt_ref[0, p, 0] = vt[p * VREG_LANES:(p + 1) * VREG_LANES, :]
    kmean_ref[0, 0] = jnp.mean(kn, axis=0, keepdims=True)


def _attn_prep(proj, gq_t, gk_t, b, t):
    nb = t // MOBA_BLOCK
    m = b * t
    row = lambda c: pl.BlockSpec((MOBA_BLOCK, D_MODEL), lambda bi, j: (bi * nb + j, c))
    vec = pl.BlockSpec((1, D_MODEL), lambda bi, j: (0, 0))
    pair = pl.BlockSpec((1, HEAD_PAIRS, MOBA_BLOCK, VREG_LANES), lambda bi, j: (bi, 0, j, 0))
    return pl.pallas_call(
        _attn_prep_kernel,
        out_shape=(
            jax.ShapeDtypeStruct((m, D_MODEL), F32),
            jax.ShapeDtypeStruct((m, D_MODEL), F32),
            jax.ShapeDtypeStruct((b, HEAD_PAIRS, t, VREG_LANES), BF16),
            jax.ShapeDtypeStruct((b, HEAD_PAIRS, t, VREG_LANES), BF16),
            jax.ShapeDtypeStruct((b, HEAD_PAIRS, nb, VREG_LANES, MOBA_BLOCK), BF16),
            jax.ShapeDtypeStruct((b, nb, 1, D_MODEL), F32),
        ),
        grid=(b, nb),
        in_specs=[row(COL_Q), row(COL_K), row(COL_V), vec, vec],
        out_specs=(
            row(0), row(0), pair, pair,
            pl.BlockSpec((1, HEAD_PAIRS, 1, VREG_LANES, MOBA_BLOCK), lambda bi, j: (bi, 0, j, 0, 0)),
            pl.BlockSpec((1, 1, 1, D_MODEL), lambda bi, j: (bi, j, 0, 0)),
        ),
        compiler_params=_cparams(2, 40),
        name="attn_prep",
    )(proj, proj, proj, gq_t, gk_t)


def _moba_kernel(q_ref, k_ref, vt_ref, km_ref, o_ref, sel_ref, *, nb):
    i = pl.program_id(2)
    q = q_ref[0, 0]
    km = km_ref[0]
    lane = _iota((1, VREG_LANES), 1)
    blk = _iota((nb, 1), 0)
    causal = _iota((MOBA_BLOCK, MOBA_BLOCK), 0) <= _iota((MOBA_BLOCK, MOBA_BLOCK), 1)
    valid = blk < i
    outs = []
    for hh in range(2):
        in_head = (lane >= hh * HEAD_DIM) & (lane < (hh + 1) * HEAD_DIM)
        qm = jnp.where(in_head, q, jnp.zeros_like(q))
        km_hi, km_lo = _split2(jnp.where(in_head, km, 0.0))
        gate = _dot_nt(km_hi, qm) + _dot_nt(km_lo, qm)
        gate = jnp.where(valid, gate, -jnp.inf)
        rank = jnp.zeros(gate.shape, F32)
        for m in range(nb):
            gm = gate[m:m + 1, :]
            beats = (gm > gate) | ((gm == gate) & (blk > m))
            rank = rank + beats.astype(F32)
        sel_ref[hh] = (valid & (rank < TOP_K)).astype(F32)

        hs = slice(hh * HEAD_DIM, (hh + 1) * HEAD_DIM)
        k_i = k_ref[0, 0, pl.ds(pl.multiple_of(i * MOBA_BLOCK, MOBA_BLOCK), MOBA_BLOCK), :]
        s = jnp.where(causal, _dot_nt(k_i, qm), MASK_NEG)
        m0 = jnp.max(s, axis=0, keepdims=True)
        p = jnp.exp(s - m0)
        l0 = jnp.sum(p, axis=0, keepdims=True)
        acc0 = _dot(vt_ref[0, 0, i, hs, :], p.astype(BF16))

        def body(j, carry, hh=hh, hs=hs, qm=qm):
            m_run, l_run, acc = carry
            k_j = k_ref[0, 0, pl.ds(pl.multiple_of(j * MOBA_BLOCK, MOBA_BLOCK), MOBA_BLOCK), :]
            picked = sel_ref[hh, pl.ds(j, 1), :] > 0.0
            s = jnp.where(picked, _dot_nt(k_j, qm), MASK_NEG)
            m_new = jnp.maximum(m_run, jnp.max(s, axis=0, keepdims=True))
            alpha = jnp.exp(m_run - m_new)
            p = jnp.exp(s - m_new)
            l_new = alpha * l_run + jnp.sum(p, axis=0, keepdims=True)
            acc = alpha * acc + _dot(vt_ref[0, 0, j, hs, :], p.astype(BF16))
            return m_new, l_new, acc

        _, l_fin, acc = lax.fori_loop(0, i, body, (m0, l0, acc0))
        outs.append(acc / l_fin)
    o_ref[0] = jnp.concatenate(outs, axis=0).T


def _moba_prompt(q_pair, k_pair, vt, kmean):
    b, _, t, _ = q_pair.shape
    nb = t // MOBA_BLOCK
    return pl.pallas_call(
        functools.partial(_moba_kernel, nb=nb),
        out_shape=jax.ShapeDtypeStruct((b, t, D_MODEL), F32),
        grid=(b, HEAD_PAIRS, nb),
        in_specs=[
            pl.BlockSpec((1, 1, MOBA_BLOCK, VREG_LANES), lambda bi, p, i: (bi, p, i, 0)),
            pl.BlockSpec((1, 1, t, VREG_LANES), lambda bi, p, i: (bi, p, 0, 0)),
            pl.BlockSpec((1, 1, nb, VREG_LANES, MOBA_BLOCK), lambda bi, p, i: (bi, p, 0, 0, 0)),
            pl.BlockSpec((1, nb, VREG_LANES), lambda bi, p, i: (bi, 0, p)),
        ],
        out_specs=pl.BlockSpec((1, MOBA_BLOCK, VREG_LANES), lambda bi, p, i: (bi, i, p)),
        scratch_shapes=[pltpu.VMEM((2, nb, MOBA_BLOCK), F32)],
        compiler_params=_cparams(3, 32),
        name="moba_prompt",
    )(q_pair, k_pair, vt, kmean)


def _group_masks():
    lane = _iota((1, MXU_LANES), 1)
    head = lax.shift_right_logical(lane, 6)
    return [head == h for h in range(HEADS_PER_GROUP)]


def _block_diag(x, masks):
    zero = jnp.zeros_like(x)
    return jnp.concatenate([jnp.where(mk, x, zero) for mk in masks], axis=0)


def _diag_blocks(g, masks):
    out = None
    for h, mk in enumerate(masks):
        part = jnp.where(mk, g[h * HEAD_DIM:(h + 1) * HEAD_DIM, :], 0.0)
        out = part if out is None else out + part
    return out


def _rwkv_prompt_kernel(pr_ref, pk_ref, pv_ref, pl_ref, mur_ref, muk_ref, muv_ref, mul_ref,
                        w0_ref, a0_ref, wl_ref, kk_ref, ka_ref, rk_ref, lnw_ref, lnb_ref,
                        o_ref, wkv_ref, prev_r, prev_k, prev_v, prev_l, h_ref):
    c = pl.program_id(1)

    @pl.when(c == 0)
    def _():
        prev_r[...] = jnp.zeros_like(prev_r)
        prev_k[...] = jnp.zeros_like(prev_k)
        prev_v[...] = jnp.zeros_like(prev_v)
        prev_l[...] = jnp.zeros_like(prev_l)
        h_ref[...] = jnp.zeros_like(h_ref)

    first_row = _iota((CHUNK, 1), 0) == 0

    def shift(x_ref, prev_ref, mu_ref):
        x = x_ref[...]
        prev = jnp.where(first_row, prev_ref[...], pltpu.roll(x, 1, 0))
        prev_ref[...] = x[CHUNK - 1:CHUNK, :]
        return x + (prev - x) * mu_ref[...]

    ur = shift(pr_ref, prev_r, mur_ref)
    uk = shift(pk_ref, prev_k, muk_ref)
    uv = shift(pv_ref, prev_v, muv_ref)
    ul = shift(pl_ref, prev_l, mul_ref)

    head_ones = _head_ones()
    e, a, g, kk, kmod = _rwkv_mix(ur, uk, ul, w0_ref[...], a0_ref[...], wl_ref[...],
                                  kk_ref[...], ka_ref[...], head_ones)
    bvec = kk * a

    tri = (_iota((CHUNK, CHUNK), 0) >= _iota((CHUNK, CHUNK), 1)).astype(BF16)
    e1, e2, e3 = _split3(e)
    cum = _dot(tri, e1) + _dot(tri, e2) + _dot(tri, e3)
    cum_end = cum[CHUNK - 1:CHUNK, :]
    dec = jnp.exp(-cum)
    grow = jnp.exp(cum)
    rest = jnp.exp(cum - cum_end)
    rt = ur * dec
    km = kk * jnp.exp(e - cum)
    kp = kmod * grow
    bp = bvec * grow
    kd = kmod * rest
    bd = bvec * rest
    gam_end = dec[CHUNK - 1:CHUNK, :]

    masks = _group_masks()
    tok = _iota((CHUNK, MXU_LANES), 0)
    col = _iota((CHUNK, MXU_LANES), 1) & (HEAD_DIM - 1)
    strict = tok > col
    incl = tok >= col
    eye = tok == col

    y_parts = []
    for grp in range(N_GROUPS):
        sl = slice(grp * MXU_LANES, (grp + 1) * MXU_LANES)
        km_g, rt_g, v_g = km[:, sl], rt[:, sl], uv[:, sl]
        kd_g, bd_g = kd[:, sl], bd[:, sl]

        lhs = jnp.concatenate([km_g, rt_g], axis=0).astype(BF16)
        s1 = _dot_nt(lhs, _block_diag(kp[:, sl].astype(BF16), masks))
        s2 = _dot_nt(lhs, _block_diag(bp[:, sl].astype(BF16), masks))
        a_kk = jnp.where(strict, s1[:CHUNK], 0.0)
        a_rk = jnp.where(incl, s1[CHUNK:], 0.0)
        a_kb = jnp.where(strict, s2[:CHUNK], 0.0)
        a_rb = jnp.where(incl, s2[CHUNK:], 0.0)

        pw = a_kb
        yinv = -a_kb
        pw_bd = _block_diag(pw.astype(BF16), masks)
        for _ in range(5):
            pw = _dot(pw.astype(BF16), pw_bd)
            pw_bd = _block_diag(pw.astype(BF16), masks)
            yinv = yinv + pw + _dot(yinv.astype(BF16), pw_bd)

        av = _dot(jnp.concatenate([a_kk, a_rk], axis=0).astype(BF16),
                  _block_diag(v_g.astype(BF16), masks))
        q1, q2 = av[:CHUNK], av[CHUNK:]
        yinv_b = yinv.astype(BF16)
        km2 = km_g + _dot(yinv_b, _block_diag(km_g.astype(BF16), masks))
        u0 = q1 + _dot(yinv_b, _block_diag(q1.astype(BF16), masks))
        a_rb_b = a_rb.astype(BF16)
        rt2 = rt_g - _dot(a_rb_b, _block_diag(km2.astype(BF16), masks))
        o0 = q2 - _dot(a_rb_b, _block_diag(u0.astype(BF16), masks))

        kd_b, bd_b = kd_g.astype(BF16), bd_g.astype(BF16)
        n_c = (_diag_blocks(_dot_tn(kd_b, v_g.astype(BF16)), masks)
               - _diag_blocks(_dot_tn(bd_b, u0.astype(BF16)), masks))
        m_c = (jnp.where(eye, gam_end[:, sl], 0.0)
               - _diag_blocks(_dot_tn(bd_b, km2.astype(BF16)), masks))

        h_hi, h_lo = _split2(h_ref[:, sl])
        h_hi, h_lo = _block_diag(h_hi, masks), _block_diag(h_lo, masks)
        l_hi, l_lo = _split2(jnp.concatenate([rt2, m_c], axis=0))
        res = _dot(l_hi, h_hi) + _dot(l_hi, h_lo) + _dot(l_lo, h_hi)
        y_parts.append(o0 + res[:CHUNK])
        h_ref[:, sl] = res[CHUNK:] + n_c

    y = jnp.concatenate(y_parts, axis=1)
    o_ref[...] = _rwkv_out(y, ur, kmod, uv, g, rk_ref[...], lnw_ref[...], lnb_ref[...], head_ones)

    @pl.when(c == pl.num_programs(1) - 1)
    def _():
        wkv_ref[0] = h_ref[...].T


def _rwkv_prompt(proj, mu_r, mu_k, mu_v, mu_l, w0, a0, w_lora, k_k, k_a, r_k, ln_w, ln_b, b, t):
    nc = t // CHUNK
    row = lambda c: pl.BlockSpec((CHUNK, D_MODEL), lambda bi, ci: (bi * nc + ci, c))
    vec = pl.BlockSpec((1, D_MODEL), lambda bi, ci: (0, 0))
    vec_l = pl.BlockSpec((1, LORA_W), lambda bi, ci: (0, 0))
    return pl.pallas_call(
        _rwkv_prompt_kernel,
        out_shape=(
            jax.ShapeDtypeStruct((b * t, D_MODEL), F32),
            jax.ShapeDtypeStruct((b, D_MODEL, HEAD_DIM), F32),
        ),
        grid=(b, nc),
        in_specs=[
            row(COL_R), row(COL_RK), row(COL_RV),
            pl.BlockSpec((CHUNK, LORA_W), lambda bi, ci: (bi * nc + ci, COL_LORA)),
            vec, vec, vec, vec_l, vec, vec,
            pl.BlockSpec((LORA_W, 3 * D_MODEL), lambda bi, ci: (0, 0)),
            vec, vec, vec, vec, vec,
        ],
        out_specs=(
            row(0),
            pl.BlockSpec((1, D_MODEL, HEAD_DIM), lambda bi, ci: (bi, 0, 0)),
        ),
        scratch_shapes=[
            pltpu.VMEM((1, D_MODEL), F32), pltpu.VMEM((1, D_MODEL), F32),
            pltpu.VMEM((1, D_MODEL), F32), pltpu.VMEM((1, LORA_W), F32),
            pltpu.VMEM((HEAD_DIM, D_MODEL), F32),
        ],
        compiler_params=_cparams(2, 40),
        name="rwkv_prompt",
    )(proj, proj, proj, proj, mu_r, mu_k, mu_v, mu_l, w0, a0, w_lora, k_k, k_a, r_k, ln_w, ln_b)


def _merge_ffn_kernel(x_ref, ga_ref, gb_ref, oa_ref, ob_ref, wo_ref, gf_ref, wup_ref, wdn_ref, y_ref):
    mix = jax.nn.sigmoid(ga_ref[...]) * oa_ref[...] + jax.nn.sigmoid(gb_ref[...]) * ob_ref[...]
    x1 = x_ref[...] + _dot(mix.astype(BF16), wo_ref[...])
    h = _rmsnorm(x1, gf_ref[...]).astype(BF16)
    acc = x1
    for c in range(D_FF // D_MODEL):
        sl = slice(c * D_MODEL, (c + 1) * D_MODEL)
        f = jnp.maximum(_dot(h, wup_ref[:, sl]), 0.0)
        acc = acc + _dot((f * f).astype(BF16), wdn_ref[sl, :])
    y_ref[...] = acc


def _merge_ffn(x, proj, o_a, o_b, w_o, g_ffn, w_up, w_down):
    m = x.shape[0]
    tm = min(256, m)
    row = lambda c: pl.BlockSpec((tm, D_MODEL), lambda i: (i, c))
    once = pl.Buffered(1)
    return pl.pallas_call(
        _merge_ffn_kernel,
        out_shape=jax.ShapeDtypeStruct((m, D_MODEL), F32),
        grid=(m // tm,),
        in_specs=[
            row(0), row(COL_GA), row(COL_GB), row(0), row(0),
            pl.BlockSpec((D_MODEL, D_MODEL), lambda i: (0, 0), pipeline_mode=once),
            pl.BlockSpec((1, D_MODEL), lambda i: (0, 0)),
            pl.BlockSpec((D_MODEL, D_FF), lambda i: (0, 0), pipeline_mode=once),
            pl.BlockSpec((D_FF, D_MODEL), lambda i: (0, 0), pipeline_mode=once),
        ],
        out_specs=row(0),
        compiler_params=_cparams(1, 48),
        name="merge_ffn",
    )(x, proj, proj, o_a, o_b, w_o, g_ffn, w_up, w_down)


def _sample_prep_kernel(p_ref, sr_ref, sk_ref, sv_ref, sl_ref, gq_ref, gk_ref,
                        mur_ref, muk_ref, muv_ref, mul_ref, w0_ref, a0_ref, wl_ref, kk_ref, ka_ref,
                        q_ref, kn_ref, r_ref, dec_ref, kmod_ref, v_ref, kkn_ref, b_ref, g_ref):
    head_ones = _head_ones()
    col = lambda c: p_ref[:, c * D_MODEL:(c + 1) * D_MODEL]
    q_ref[...] = _head_rmsnorm(col(COL_Q), gq_ref[...], head_ones)
    kn_ref[...] = _head_rmsnorm(col(COL_K), gk_ref[...], head_ones)

    def mix(x, prev, mu):
        return x + (prev - x) * mu

    ur = mix(col(COL_R), sr_ref[...], mur_ref[...])
    uk = mix(col(COL_RK), sk_ref[...], muk_ref[...])
    uv = mix(col(COL_RV), sv_ref[...], muv_ref[...])
    ul = mix(p_ref[:, COL_LORA * LORA_W:], sl_ref[...], mul_ref[...])
    e, a, g, kk, kmod = _rwkv_mix(ur, uk, ul, w0_ref[...], a0_ref[...], wl_ref[...],
                                  kk_ref[...], ka_ref[...], head_ones)
    r_ref[...] = ur
    dec_ref[...] = jnp.exp(-e)
    kmod_ref[...] = kmod
    v_ref[...] = uv
    kkn_ref[...] = kk
    b_ref[...] = kk * a
    g_ref[...] = g


def _sample_prep(proj_s, shift_r, shift_k, shift_v, shift_l, gq_t, gk_t,
                 mu_r, mu_k, mu_v, mu_l, w0, a0, w_lora, k_k, k_a):
    n = proj_s.shape[0]
    out = jax.ShapeDtypeStruct((n, D_MODEL), F32)
    return pl.pallas_call(
        _sample_prep_kernel,
        out_shape=(out,) * 9,
        compiler_params=_cparams(0, 40),
        name="sample_prep",
    )(proj_s, shift_r, shift_k, shift_v, shift_l, gq_t, gk_t,
      mu_r, mu_k, mu_v, mu_l, w0, a0, w_lora, k_k, k_a)


PAGES_PER_STEP = 8


def _page_means_kernel(pt_ref, *refs):
    page_refs, o_ref = refs[:PAGES_PER_STEP], refs[PAGES_PER_STEP]
    for n in range(PAGES_PER_STEP // PAGES_PER_BLOCK):
        tot = None
        for e in range(PAGES_PER_BLOCK):
            s = jnp.sum(page_refs[n * PAGES_PER_BLOCK + e][0, 0], axis=0)
            tot = s if tot is None else tot + s
        o_ref[0, n] = tot * (1.0 / MOBA_BLOCK)


def _page_block_means(cache_k, page_table):
    bs, n_pages = page_table.shape

    def page_spec(e):
        return pl.BlockSpec(
            (1, 1, PAGE_ROWS, HEADS, HEAD_DIM),
            lambda bi, s, pt: (0, pt[bi * n_pages + s * PAGES_PER_STEP + e], 0, 0, 0))

    blocks_per_step = PAGES_PER_STEP // PAGES_PER_BLOCK
    return pl.pallas_call(
        _page_means_kernel,
        out_shape=jax.ShapeDtypeStruct((bs, n_pages // PAGES_PER_BLOCK, HEADS, HEAD_DIM), F32),
        grid_spec=pltpu.PrefetchScalarGridSpec(
            num_scalar_prefetch=1,
            grid=(bs, n_pages // PAGES_PER_STEP),
            in_specs=[page_spec(e) for e in range(PAGES_PER_STEP)],
            out_specs=pl.BlockSpec((1, blocks_per_step, HEADS, HEAD_DIM), lambda bi, s, pt: (bi, s, 0, 0)),
        ),
        compiler_params=_cparams(2, 32),
        name="page_block_means",
    )(page_table.reshape(-1), *([cache_k] * PAGES_PER_STEP))


def _sample_select_kernel(km_ref, q_ref, pe_ref, po_ref, o_ref, *, nb):
    km = km_ref[0]
    q = q_ref[0]
    gate = jnp.sum(km * q[None], axis=-1, keepdims=True)
    blk = _iota((nb, 1, 1), 0)
    rank = jnp.zeros(gate.shape, F32)
    for m in range(nb):
        gm = gate[m:m + 1]
        beats = (gm > gate) | ((gm == gate) & (blk > m))
        rank = rank + beats.astype(F32)
    lane = _iota((1, 2 * TOP_K + 2), 1)
    out = jnp.zeros((HEADS, 2 * TOP_K + 2), F32)
    for s in range(TOP_K):
        slot = (rank == s).astype(F32)
        for e, p_ref in enumerate((pe_ref, po_ref)):
            page = jnp.sum(slot * p_ref[0], axis=0)
            out = out + page * (lane == 2 * s + e).astype(F32)
    o_ref[0] = out.astype(I32)


def _sample_select(kmean_s, q_s, pages_even, pages_odd):
    bs, nb = kmean_s.shape[:2]
    return pl.pallas_call(
        functools.partial(_sample_select_kernel, nb=nb),
        out_shape=jax.ShapeDtypeStruct((bs, HEADS, 2 * TOP_K + 2), I32),
        grid=(bs,),
        in_specs=[
            pl.BlockSpec((1, nb, HEADS, HEAD_DIM), lambda bi: (bi, 0, 0, 0)),
            pl.BlockSpec((1, HEADS, HEAD_DIM), lambda bi: (bi, 0, 0)),
            pl.BlockSpec((1, nb, 1, 1), lambda bi: (bi, 0, 0, 0)),
            pl.BlockSpec((1, nb, 1, 1), lambda bi: (bi, 0, 0, 0)),
        ],
        out_specs=pl.BlockSpec((1, HEADS, 2 * TOP_K + 2), lambda bi: (bi, 0, 0)),
        compiler_params=_cparams(1, 32),
        name="sample_select",
    )(kmean_s, q_s, pages_even, pages_odd)


SLOTS = 8
N_SEL_PAGES = PAGES_PER_BLOCK * TOP_K


def _sample_attn_kernel(pages_ref, q_ref, kn_ref, vn_ref, ck_ref, cv_ref, o_ref, kbuf, vbuf, sem):
    bi = pl.program_id(0)

    @pl.when(bi == 0)
    def _():
        kbuf[...] = jnp.zeros_like(kbuf)
        vbuf[...] = jnp.zeros_like(vbuf)

    def copies(h):
        out = []
        for j in range(N_SEL_PAGES):
            page = pages_ref[(bi * HEADS + h) * SLOTS + j]
            for src, dst, which in ((ck_ref, kbuf, 0), (cv_ref, vbuf, 1)):
                out.append(pltpu.make_async_copy(
                    src.at[0, page, :, pl.ds(h, 1), :],
                    dst.at[h, :, pl.ds(j, 1), :],
                    sem.at[which, h]))
        return out

    for h in range(HEADS):
        for cp in copies(h):
            cp.start()

    q = q_ref[0]
    q_b = q.astype(BF16)
    s_new = jnp.sum(q_b.astype(F32) * kn_ref[0].astype(BF16).astype(F32), axis=-1, keepdims=True) * ATT_SCALE
    v_new = vn_ref[0].astype(BF16).astype(F32)
    slot_ok = (_iota((1, PAGE_ROWS * SLOTS), 1) & (SLOTS - 1)) < N_SEL_PAGES

    for h in range(HEADS):
        for cp in copies(h):
            cp.wait()
        k2 = kbuf[h].reshape(PAGE_ROWS * SLOTS, HEAD_DIM).astype(BF16)
        v2 = vbuf[h].reshape(PAGE_ROWS * SLOTS, HEAD_DIM).astype(BF16)
        q8 = jnp.broadcast_to(q_b[h:h + 1, :], (8, HEAD_DIM))
        s = _dot_nt(q8, k2)[0:1, :] * ATT_SCALE
        s = jnp.where(slot_ok, s, MASK_NEG)
        sn = s_new[h:h + 1, :]
        m = jnp.maximum(jnp.max(s, axis=-1, keepdims=True), sn)
        p = jnp.exp(s - m)
        pn = jnp.exp(sn - m)
        denom = jnp.sum(p, axis=-1, keepdims=True) + pn
        p8 = jnp.broadcast_to(p.astype(BF16), (8, PAGE_ROWS * SLOTS))
        acc = _dot(p8, v2)[0:1, :] + pn.astype(BF16).astype(F32) * v_new[h:h + 1, :]
        o_ref[0, h:h + 1, :] = acc / denom


def _sample_attn(pages, q_s, k_new, v_new, cache_k, cache_v):
    bs = q_s.shape[0]
    vec = pl.BlockSpec((1, HEADS, HEAD_DIM), lambda bi, pg: (bi, 0, 0))
    return pl.pallas_call(
        _sample_attn_kernel,
        out_shape=jax.ShapeDtypeStruct((bs, HEADS, HEAD_DIM), F32),
        grid_spec=pltpu.PrefetchScalarGridSpec(
            num_scalar_prefetch=1,
            grid=(bs,),
            in_specs=[vec, vec, vec, pl.BlockSpec(memory_space=pl.ANY), pl.BlockSpec(memory_space=pl.ANY)],
            out_specs=vec,
            scratch_shapes=[
                pltpu.VMEM((HEADS, PAGE_ROWS, SLOTS, HEAD_DIM), F32),
                pltpu.VMEM((HEADS, PAGE_ROWS, SLOTS, HEAD_DIM), F32),
                pltpu.SemaphoreType.DMA((2, HEADS)),
            ],
        ),
        compiler_params=_cparams(1, 40),
        name="sample_attn",
    )(pages.reshape(-1), q_s, k_new, v_new, cache_k, cache_v)


def _sample_wkv_kernel(s_ref, dec_ref, kk_ref, b_ref, k_ref, v_ref, r_ref, so_ref, y_ref):
    st = s_ref[0]
    eye = (_iota((HEAD_DIM, HEAD_DIM), 0) == _iota((HEAD_DIM, HEAD_DIM), 1)).astype(F32)[None]
    sa = jnp.sum(st * kk_ref[0], axis=-1, keepdims=True)
    v_col = jnp.sum(v_ref[0] * eye, axis=-1, keepdims=True)
    s_new = st * dec_ref[0] - sa * b_ref[0] + v_col * k_ref[0]
    so_ref[0] = s_new
    y_col = jnp.sum(s_new * r_ref[0], axis=-1, keepdims=True)
    y_ref[0] = jnp.sum(y_col * eye, axis=1, keepdims=True)


def _sample_wkv(state, dec, kk, bvec, kmod, v, r):
    bs = state.shape[0]
    vec = pl.BlockSpec((1, HEADS, 1, HEAD_DIM), lambda bi: (bi, 0, 0, 0))
    st = pl.BlockSpec((1, HEADS, HEAD_DIM, HEAD_DIM), lambda bi: (bi, 0, 0, 0))
    as_vec = lambda x: x.reshape(bs, HEADS, 1, HEAD_DIM)
    return pl.pallas_call(
        _sample_wkv_kernel,
        out_shape=(
            jax.ShapeDtypeStruct(state.shape, F32),
            jax.ShapeDtypeStruct((bs, HEADS, 1, HEAD_DIM), F32),
        ),
        grid=(bs,),
        in_specs=[st] + [vec] * 6,
        out_specs=(st, vec),
        compiler_params=_cparams(1, 32),
        name="sample_wkv",
    )(state, as_vec(dec), as_vec(kk), as_vec(bvec), as_vec(kmod), as_vec(v), as_vec(r))


def _sample_rwkv_out_kernel(y_ref, r_ref, kmod_ref, v_ref, g_ref, rk_ref, lnw_ref, lnb_ref, o_ref):
    o_ref[...] = _rwkv_out(y_ref[...], r_ref[...], kmod_ref[...], v_ref[...], g_ref[...],
                           rk_ref[...], lnw_ref[...], lnb_ref[...], _head_ones())


def _sample_rwkv_out(y, r, kmod, v, g, r_k, ln_w, ln_b):
    return pl.pallas_call(
        _sample_rwkv_out_kernel,
        out_shape=jax.ShapeDtypeStruct(y.shape, F32),
        compiler_params=_cparams(0, 32),
        name="sample_rwkv_out",
    )(y, r, kmod, v, g, r_k, ln_w, ln_b)


def kernel(x_prompt, x_sample, cache_k, cache_v, state_wkv, state_shift, page_table, g_mix, w_in, g_q, g_k,
           mu_shift, w0, w_w2, a0, w_a2, w_g2, k_k, k_a, r_k, ln_x_w, ln_x_b, w_o, g_ffn, w_up, w_down):
    b, t, d = x_prompt.shape
    bs, ts, _ = x_sample.shape
    depth = w_in.shape[0]
    assert depth == 1 and ts == 1 and d == D_MODEL
    assert t % MOBA_BLOCK == 0 and page_table.shape[1] % PAGES_PER_STEP == 0
    rwkv_w = 3 * D_MODEL
    off_rwkv, off_gate = 3 * D_MODEL, 6 * D_MODEL + LORA_W
    assert w_in.shape[2] == PROJ_W

    w = w_in[0]
    w_perm = jnp.concatenate(
        [w[:, :off_rwkv + rwkv_w], w[:, off_gate:], w[:, off_rwkv + rwkv_w:off_gate]], axis=1).astype(BF16)
    w_lora = jnp.zeros((LORA_W, 3 * D_MODEL), F32)
    w_lora = w_lora.at[:DECAY_LORA, :D_MODEL].set(w_w2[0])
    w_lora = w_lora.at[DECAY_LORA:DECAY_LORA + AAA_LORA, D_MODEL:2 * D_MODEL].set(w_a2[0])
    w_lora = w_lora.at[DECAY_LORA + AAA_LORA:, 2 * D_MODEL:].set(w_g2[0]).astype(BF16)
    vec = lambda x: x.reshape(1, -1).astype(F32)
    gq_t = vec(jnp.tile(g_q[0], HEADS))
    gk_t = vec(jnp.tile(g_k[0], HEADS))
    mu = mu_shift[0]
    mu_r, mu_k, mu_v, mu_l = (vec(mu[:D_MODEL]), vec(mu[D_MODEL:2 * D_MODEL]),
                              vec(mu[2 * D_MODEL:rwkv_w]), vec(mu[rwkv_w:]))
    rw = (vec(w0[0]), vec(a0[0]), w_lora, vec(k_k[0]), vec(k_a[0]))
    r_k_v, ln_w, ln_b = vec(r_k[0]), vec(ln_x_w[0]), vec(ln_x_b[0])
    g_mix_v, g_ffn_v = vec(g_mix[0]), vec(g_ffn[0])
    w_o_b, w_up_b, w_down_b = w_o[0].astype(BF16), w_up[0].astype(BF16), w_down[0].astype(BF16)

    def shift_rows(p_rows):
        return jnp.concatenate([p_rows[:, COL_R * D_MODEL:(COL_RV + 1) * D_MODEL],
                                p_rows[:, COL_LORA * LORA_W:]], axis=-1)

    xp = x_prompt.reshape(b * t, d)
    proj_p = _project(xp, g_mix_v, w_perm)
    k_p, v_p, q_pair, k_pair, vt, kmean = _attn_prep(proj_p, gq_t, gk_t, b, t)
    o_att = _moba_prompt(q_pair, k_pair, vt, kmean.reshape(b, t // MOBA_BLOCK, D_MODEL))
    o_rwkv, wkv_t = _rwkv_prompt(proj_p, mu_r, mu_k, mu_v, mu_l, *rw, r_k_v, ln_w, ln_b, b, t)
    y_p = _merge_ffn(xp, proj_p, o_att.reshape(b * t, d), o_rwkv, w_o_b, g_ffn_v, w_up_b, w_down_b)
    shift_p = shift_rows(proj_p.reshape(b, t, PROJ_W)[:, -1])

    xs = x_sample.reshape(bs, d)
    proj_s = _project(xs, g_mix_v, w_perm)
    sh = state_shift[0]
    (q_s, kn_s, r_s, dec_s, kmod_s, v_s, kk_s, b_s, g_s) = _sample_prep(
        proj_s, sh[:, :D_MODEL], sh[:, D_MODEL:2 * D_MODEL], sh[:, 2 * D_MODEL:rwkv_w], sh[:, rwkv_w:],
        gq_t, gk_t, mu_r, mu_k, mu_v, mu_l, *rw)
    v_new = proj_s[:, COL_V * D_MODEL:(COL_V + 1) * D_MODEL]
    heads = lambda x: x.reshape(bs, HEADS, HEAD_DIM)
    kmean_s = _page_block_means(cache_k, page_table)
    n_blk = page_table.shape[1] // PAGES_PER_BLOCK
    pt_f = page_table.astype(F32).reshape(bs, n_blk, PAGES_PER_BLOCK)
    pages = _sample_select(kmean_s, heads(q_s), pt_f[:, :, 0].reshape(bs, n_blk, 1, 1),
                           pt_f[:, :, 1].reshape(bs, n_blk, 1, 1))
    o_att_s = _sample_attn(pages, heads(q_s), heads(kn_s), heads(v_new), cache_k, cache_v)
    wkv_s, y_s = _sample_wkv(state_wkv[0], dec_s, kk_s, b_s, kmod_s, v_s, r_s)
    o_rwkv_s = _sample_rwkv_out(y_s.reshape(bs, d), r_s, kmod_s, v_s, g_s, r_k_v, ln_w, ln_b)
    y_samp = _merge_ffn(xs, proj_s, o_att_s.reshape(bs, d), o_rwkv_s, w_o_b, g_ffn_v, w_up_b, w_down_b)

    hd = (HEADS, HEAD_DIM)
    return (
        y_p.reshape(b, t, d),
        y_samp.reshape(bs, ts, d),
        k_p.reshape(1, b, t, *hd),
        v_p.reshape(1, b, t, *hd),
        kn_s.reshape(1, bs, ts, *hd),
        v_new.reshape(1, bs, ts, *hd),
        wkv_t.reshape(1, b, HEADS, HEAD_DIM, HEAD_DIM),
        wkv_s[None],
        shift_p[None],
        shift_rows(proj_s)[None],
    )
```

```python
import functools

import jax
import jax.numpy as jnp
from jax import lax
from jax.experimental import pallas as pl
from jax.experimental.pallas import tpu as pltpu

F32 = jnp.float32
BF16 = jnp.bfloat16
I32 = jnp.int32

D_MODEL = 1024
HEADS = 16
HEAD_DIM = 64
MOBA_BLOCK = 256
TOP_K = 3
PAGE_ROWS = 128
PAGES_PER_BLOCK = MOBA_BLOCK // PAGE_ROWS
ATT_SCALE = HEAD_DIM ** -0.5
DECAY_LORA = 64
AAA_LORA = 64
GATE_LORA = 128
LORA_W = DECAY_LORA + AAA_LORA + GATE_LORA
D_FF = 4 * D_MODEL
NORM_EPS = 1e-6
GN_EPS = 6.4e-4
EXP_M05 = 0.6065306597126334
MASK_NEG = -1e30

COL_Q, COL_K, COL_V, COL_R, COL_RK, COL_RV, COL_GA, COL_GB = range(8)
COL_LORA = 8 * D_MODEL // LORA_W
PROJ_W = 8 * D_MODEL + LORA_W

MXU_LANES = 256
VREG_LANES = 128
HEADS_PER_GROUP = MXU_LANES // HEAD_DIM
N_GROUPS = D_MODEL // MXU_LANES
HEAD_PAIRS = D_MODEL // VREG_LANES
CHUNK = 64
assert CHUNK == HEAD_DIM
CHUNKS_PER_STEP = 2
STEP_ROWS = CHUNKS_PER_STEP * CHUNK
VMEM_LIMIT_CAP = 56 * 1024 * 1024


def _cparams(n_axes, vmem_mib):
    return pltpu.CompilerParams(
        dimension_semantics=("arbitrary",) * n_axes,
        vmem_limit_bytes=min(vmem_mib * 1024 * 1024, VMEM_LIMIT_CAP),
    )


def _dot(a, b):
    return jnp.dot(a, b, preferred_element_type=F32)


def _dot_nt(a, b):
    return lax.dot_general(a, b, (((1,), (1,)), ((), ())), preferred_element_type=F32)


def _dot_tn(a, b):
    return lax.dot_general(a, b, (((0,), (0,)), ((), ())), preferred_element_type=F32)


def _split2(x):
    hi = x.astype(BF16)
    lo = (x - hi.astype(F32)).astype(BF16)
    return hi, lo


def _split3(x):
    h1 = x.astype(BF16)
    r1 = x - h1.astype(F32)
    h2 = r1.astype(BF16)
    h3 = (r1 - h2.astype(F32)).astype(BF16)
    return h1, h2, h3


def _iota(shape, axis):
    return lax.broadcasted_iota(I32, shape, axis)


def _head_ones():
    r = lax.shift_right_logical(_iota((MXU_LANES, MXU_LANES), 0), 6)
    c = lax.shift_right_logical(_iota((MXU_LANES, MXU_LANES), 1), 6)
    return (r == c).astype(BF16)


def _head_sum(x, head_ones):
    hi, lo = _split2(x)
    parts = []
    for g in range(N_GROUPS):
        sl = slice(g * MXU_LANES, (g + 1) * MXU_LANES)
        parts.append(_dot(hi[:, sl], head_ones) + _dot(lo[:, sl], head_ones))
    return jnp.concatenate(parts, axis=1)


def _rmsnorm(x, g):
    ms = jnp.mean(x * x, axis=-1, keepdims=True)
    return x * lax.rsqrt(ms + NORM_EPS) * g


def _head_rmsnorm(x, g, head_ones):
    ss = _head_sum(x * x, head_ones)
    return x * lax.rsqrt(ss * (1.0 / HEAD_DIM) + NORM_EPS) * g


def _rwkv_mix(ur, uk, ul, w0, a0, w_lora, k_k, k_a, head_ones):
    lane = _iota((1, LORA_W), 1)
    t = jnp.where(lane < DECAY_LORA, jnp.tanh(ul),
                  jnp.where(lane < DECAY_LORA + AAA_LORA, ul, jax.nn.sigmoid(ul)))
    lo = _dot(t.astype(BF16), w_lora)
    e = EXP_M05 * jax.nn.sigmoid(w0 + lo[:, :D_MODEL])
    a = jax.nn.sigmoid(a0 + lo[:, D_MODEL:2 * D_MODEL])
    g = lo[:, 2 * D_MODEL:]
    kk = uk * k_k
    kk = kk * lax.rsqrt(jnp.maximum(_head_sum(kk * kk, head_ones), 1e-24))
    kmod = uk * (1.0 + (a - 1.0) * k_a)
    return e, a, g, kk, kmod


def _rwkv_out(y, ur, kmod, uv, g, r_k, ln_w, ln_b, head_ones):
    inv = 1.0 / HEAD_DIM
    mean = _head_sum(y, head_ones) * inv
    d = y - mean
    var = _head_sum(d * d, head_ones) * inv
    yn = d * lax.rsqrt(var + GN_EPS) * ln_w + ln_b
    bonus = _head_sum(ur * kmod * r_k, head_ones) * uv
    return (yn + bonus) * g


def _proj_kernel(x_ref, g_ref, w_ref, o_ref, h_ref):
    @pl.when(pl.program_id(1) == 0)
    def _():
        h_ref[...] = _rmsnorm(x_ref[...], g_ref[...]).astype(BF16)

    o_ref[...] = _dot(h_ref[...], w_ref[...])


def _project(x, g_mix, w_perm):
    m = x.shape[0]
    tm = min(1024, m)
    tn = PROJ_W // 6
    return pl.pallas_call(
        _proj_kernel,
        out_shape=jax.ShapeDtypeStruct((m, PROJ_W), F32),
        grid=(m // tm, PROJ_W // tn),
        in_specs=[
            pl.BlockSpec((tm, D_MODEL), lambda i, j: (i, 0)),
            pl.BlockSpec((1, D_MODEL), lambda i, j: (0, 0)),
            pl.BlockSpec((D_MODEL, tn), lambda i, j: (0, j)),
        ],
        out_specs=pl.BlockSpec((tm, tn), lambda i, j: (i, j)),
        scratch_shapes=[pltpu.VMEM((tm, D_MODEL), BF16)],
        compiler_params=_cparams(2, 40),
        name="proj",
    )(x, g_mix, w_perm)


def _attn_prep_kernel(q_ref, k_ref, v_ref, gq_ref, gk_ref,
                      kout_ref, vout_ref, qp_ref, kp_ref, vt_ref, kmean_ref):
    head_ones = _head_ones()
    qn = _head_rmsnorm(q_ref[...], gq_ref[...], head_ones) * ATT_SCALE
    kn = _head_rmsnorm(k_ref[...], gk_ref[...], head_ones)
    vt_f = v_ref[...].T
    kout_ref[0] = kn.T
    vout_ref[0] = vt_f
    for p in range(HEAD_PAIRS):
        sl = slice(p * VREG_LANES, (p + 1) * VREG_LANES)
        qp_ref[0, p] = qn[:, sl].astype(BF16)
        kp_ref[0, p] = kn[:, sl].astype(BF16)
    vt = vt_f.astype(BF16)
    for p in range(HEAD_PAIRS):
        vt_ref[0, p, 0] = vt[p * VREG_LANES:(p + 1) * VREG_LANES, :]
    kmean_ref[0, 0] = jnp.mean(kn, axis=0, keepdims=True)


def _attn_prep(proj, gq_t, gk_t, b, t):
    nb = t // MOBA_BLOCK
    row = lambda c: pl.BlockSpec((MOBA_BLOCK, D_MODEL), lambda bi, j: (bi * nb + j, c))
    vec = pl.BlockSpec((1, D_MODEL), lambda bi, j: (0, 0))
    pair = pl.BlockSpec((1, HEAD_PAIRS, MOBA_BLOCK, VREG_LANES), lambda bi, j: (bi, 0, j, 0))
    chan_major = pl.BlockSpec((1, D_MODEL, MOBA_BLOCK), lambda bi, j: (bi, 0, j))
    return pl.pallas_call(
        _attn_prep_kernel,
        out_shape=(
            jax.ShapeDtypeStruct((b, D_MODEL, t), F32),
            jax.ShapeDtypeStruct((b, D_MODEL, t), F32),
            jax.ShapeDtypeStruct((b, HEAD_PAIRS, t, VREG_LANES), BF16),
            jax.ShapeDtypeStruct((b, HEAD_PAIRS, t, VREG_LANES), BF16),
            jax.ShapeDtypeStruct((b, HEAD_PAIRS, nb, VREG_LANES, MOBA_BLOCK), BF16),
            jax.ShapeDtypeStruct((b, nb, 1, D_MODEL), F32),
        ),
        grid=(b, nb),
        in_specs=[row(COL_Q), row(COL_K), row(COL_V), vec, vec],
        out_specs=(
            chan_major, chan_major, pair, pair,
            pl.BlockSpec((1, HEAD_PAIRS, 1, VREG_LANES, MOBA_BLOCK), lambda bi, j: (bi, 0, j, 0, 0)),
            pl.BlockSpec((1, 1, 1, D_MODEL), lambda bi, j: (bi, j, 0, 0)),
        ),
        compiler_params=_cparams(2, 40),
        name="attn_prep",
    )(proj, proj, proj, gq_t, gk_t)


def _moba_kernel(q_ref, k_ref, vt_ref, km_ref, o_ref, sel_ref, *, nb):
    i = pl.program_id(2)
    q = q_ref[0, 0]
    km = km_ref[0]
    lane = _iota((1, VREG_LANES), 1)
    blk = _iota((nb, 1), 0)
    causal = _iota((MOBA_BLOCK, MOBA_BLOCK), 0) <= _iota((MOBA_BLOCK, MOBA_BLOCK), 1)
    valid = blk < i
    head_slices = [slice(hh * HEAD_DIM, (hh + 1) * HEAD_DIM) for hh in range(2)]
    qms, init = [], []
    for hh in range(2):
        in_head = (lane >= hh * HEAD_DIM) & (lane < (hh + 1) * HEAD_DIM)
        qm = jnp.where(in_head, q, jnp.zeros_like(q))
        qms.append(qm)
        km_hi, km_lo = _split2(jnp.where(in_head, km, 0.0))
        gate = _dot_nt(km_hi, qm) + _dot_nt(km_lo, qm)
        gate = jnp.where(valid, gate, -jnp.inf)
        rank = jnp.zeros(gate.shape, F32)
        for m in range(nb):
            gm = gate[m:m + 1, :]
            beats = (gm > gate) | ((gm == gate) & (blk > m))
            rank = rank + beats.astype(F32)
        sel_ref[hh] = (valid & (rank < TOP_K)).astype(F32)

        k_i = k_ref[0, 0, pl.ds(pl.multiple_of(i * MOBA_BLOCK, MOBA_BLOCK), MOBA_BLOCK), :]
        s = jnp.where(causal, _dot_nt(k_i, qm), MASK_NEG)
        m0 = jnp.max(s, axis=0, keepdims=True)
        p = jnp.exp(s - m0)
        l0 = jnp.sum(p, axis=0, keepdims=True)
        acc0 = _dot(vt_ref[0, 0, i, head_slices[hh], :], p.astype(BF16))
        init += [m0, l0, acc0]

    def body(jj, carry):
        j0 = 2 * jj
        j1 = j0 + 1
        k01 = k_ref[0, 0, pl.ds(pl.multiple_of(j0 * MOBA_BLOCK, MOBA_BLOCK), 2 * MOBA_BLOCK), :]
        out = []
        for hh in range(2):
            m_run, l_run, acc = carry[3 * hh:3 * hh + 3]
            pick0 = sel_ref[hh, pl.ds(j0, 1), :] > 0.0
            pick1 = (sel_ref[hh, pl.ds(j1, 1), :] > 0.0) & (j1 < i)
            s = _dot_nt(k01, qms[hh])
            s0 = jnp.where(pick0, s[:MOBA_BLOCK], MASK_NEG)
            s1 = jnp.where(pick1, s[MOBA_BLOCK:], MASK_NEG)
            m_new = jnp.maximum(m_run, jnp.maximum(jnp.max(s0, axis=0, keepdims=True),
                                                   jnp.max(s1, axis=0, keepdims=True)))
            alpha = jnp.exp(m_run - m_new)
            p0 = jnp.exp(s0 - m_new)
            p1 = jnp.exp(s1 - m_new)
            l_new = alpha * l_run + jnp.sum(p0, axis=0, keepdims=True) + jnp.sum(p1, axis=0, keepdims=True)
            hs = head_slices[hh]
            acc = (alpha * acc + _dot(vt_ref[0, 0, j0, hs, :], p0.astype(BF16))
                   + _dot(vt_ref[0, 0, j1, hs, :], p1.astype(BF16)))
            out += [m_new, l_new, acc]
        return tuple(out)

    fin = lax.fori_loop(0, lax.shift_right_logical(i + 1, 1), body, tuple(init))
    o_ref[0] = jnp.concatenate([fin[2] / fin[1], fin[5] / fin[4]], axis=0).T


def _moba_prompt(q_pair, k_pair, vt, kmean):
    b, _, t, _ = q_pair.shape
    nb = t // MOBA_BLOCK
    return pl.pallas_call(
        functools.partial(_moba_kernel, nb=nb),
        out_shape=jax.ShapeDtypeStruct((b, t, D_MODEL), F32),
        grid=(b, HEAD_PAIRS, nb),
        in_specs=[
            pl.BlockSpec((1, 1, MOBA_BLOCK, VREG_LANES), lambda bi, p, i: (bi, p, i, 0)),
            pl.BlockSpec((1, 1, t, VREG_LANES), lambda bi, p, i: (bi, p, 0, 0)),
            pl.BlockSpec((1, 1, nb, VREG_LANES, MOBA_BLOCK), lambda bi, p, i: (bi, p, 0, 0, 0)),
            pl.BlockSpec((1, nb, VREG_LANES), lambda bi, p, i: (bi, 0, p)),
        ],
        out_specs=pl.BlockSpec((1, MOBA_BLOCK, VREG_LANES), lambda bi, p, i: (bi, i, p)),
        scratch_shapes=[pltpu.VMEM((2, nb, MOBA_BLOCK), F32)],
        compiler_params=_cparams(3, 32),
        name="moba_prompt",
    )(q_pair, k_pair, vt, kmean)


def _group_masks():
    lane = _iota((1, MXU_LANES), 1)
    head = lax.shift_right_logical(lane, 6)
    return [head == h for h in range(HEADS_PER_GROUP)]


def _block_diag(x, masks):
    zero = jnp.zeros_like(x)
    return jnp.concatenate([jnp.where(mk, x, zero) for mk in masks], axis=0)


def _diag_blocks(g, masks):
    out = None
    for h, mk in enumerate(masks):
        part = jnp.where(mk, g[h * HEAD_DIM:(h + 1) * HEAD_DIM, :], 0.0)
        out = part if out is None else out + part
    return out


def _rwkv_prompt_kernel(pr_ref, pk_ref, pv_ref, pl_ref, mur_ref, muk_ref, muv_ref, mul_ref,
                        w0_ref, a0_ref, wl_ref, kk_ref, ka_ref, rk_ref, lnw_ref, lnb_ref,
                        o_ref, wkv_ref, prev_r, prev_k, prev_v, prev_l, h_ref):
    c = pl.program_id(1)

    @pl.when(c == 0)
    def _():
        prev_r[...] = jnp.zeros_like(prev_r)
        prev_k[...] = jnp.zeros_like(prev_k)
        prev_v[...] = jnp.zeros_like(prev_v)
        prev_l[...] = jnp.zeros_like(prev_l)
        h_ref[...] = jnp.zeros_like(h_ref)

    first_row = _iota((STEP_ROWS, 1), 0) == 0

    def shift(x_ref, prev_ref, mu_ref):
        x = x_ref[...]
        prev = jnp.where(first_row, prev_ref[...], pltpu.roll(x, 1, 0))
        prev_ref[...] = x[STEP_ROWS - 1:STEP_ROWS, :]
        return x + (prev - x) * mu_ref[...]

    ur = shift(pr_ref, prev_r, mur_ref)
    uk = shift(pk_ref, prev_k, muk_ref)
    uv = shift(pv_ref, prev_v, muv_ref)
    ul = shift(pl_ref, prev_l, mul_ref)

    head_ones = _head_ones()
    e, a, g, kk, kmod = _rwkv_mix(ur, uk, ul, w0_ref[...], a0_ref[...], wl_ref[...],
                                  kk_ref[...], ka_ref[...], head_ones)
    bvec = kk * a

    t_row, t_col = _iota((STEP_ROWS, STEP_ROWS), 0), _iota((STEP_ROWS, STEP_ROWS), 1)
    same_chunk = lax.shift_right_logical(t_row, 6) == lax.shift_right_logical(t_col, 6)
    tri = ((t_row >= t_col) & same_chunk).astype(BF16)
    e1, e2, e3 = _split3(e)
    cum = _dot(tri, e1) + _dot(tri, e2) + _dot(tri, e3)
    chunk_rows = [slice(ci * CHUNK, (ci + 1) * CHUNK) for ci in range(CHUNKS_PER_STEP)]
    cum_end = jnp.concatenate(
        [jnp.broadcast_to(cum[rs.stop - 1:rs.stop, :], (CHUNK, D_MODEL)) for rs in chunk_rows], axis=0)
    dec = jnp.exp(-cum)
    grow = jnp.exp(cum)
    rest = jnp.exp(cum - cum_end)
    rt = ur * dec
    km = kk * jnp.exp(e - cum)
    kp = kmod * grow
    bp = bvec * grow
    kd = kmod * rest
    bd = bvec * rest

    masks = _group_masks()
    tok = _iota((CHUNK, MXU_LANES), 0)
    col = _iota((CHUNK, MXU_LANES), 1) & (HEAD_DIM - 1)
    strict = tok > col
    incl = tok >= col
    eye = tok == col
    bdiag = lambda x: _block_diag(x.astype(BF16), masks)

    lane_groups = [slice(g * MXU_LANES, (g + 1) * MXU_LANES) for g in range(N_GROUPS)]
    units = [(rs, sl) for rs in chunk_rows for sl in lane_groups]
    n_units = range(len(units))
    cut = lambda x: [x[rs, sl] for rs, sl in units]
    km_u, rt_u, v_u, kd_u, bd_u = cut(km), cut(rt), cut(uv), cut(kd), cut(bd)
    kp_bd = [bdiag(x) for x in cut(kp)]
    bp_bd = [bdiag(x) for x in cut(bp)]
    lhs = [jnp.concatenate([km_u[i], rt_u[i]], axis=0).astype(BF16) for i in n_units]
    s1 = [_dot_nt(lhs[i], kp_bd[i]) for i in n_units]
    s2 = [_dot_nt(lhs[i], bp_bd[i]) for i in n_units]
    a_kk = [jnp.where(strict, s[:CHUNK], 0.0) for s in s1]
    a_rk = [jnp.where(incl, s[CHUNK:], 0.0) for s in s1]
    a_kb = [jnp.where(strict, s[:CHUNK], 0.0) for s in s2]
    a_rb = [jnp.where(incl, s[CHUNK:], 0.0).astype(BF16) for s in s2]

    pw = a_kb
    yinv = [-x for x in a_kb]
    pw_bd = [bdiag(x) for x in pw]
    for _ in range(5):
        pw = [_dot(pw[i].astype(BF16), pw_bd[i]) for i in n_units]
        pw_bd = [bdiag(x) for x in pw]
        yinv = [yinv[i] + pw[i] + _dot(yinv[i].astype(BF16), pw_bd[i]) for i in n_units]
    yinv = [x.astype(BF16) for x in yinv]

    v_bd = [bdiag(x) for x in v_u]
    av = [_dot(jnp.concatenate([a_kk[i], a_rk[i]], axis=0).astype(BF16), v_bd[i]) for i in n_units]
    q1 = [x[:CHUNK] for x in av]
    q2 = [x[CHUNK:] for x in av]
    km2 = [km_u[i] + _dot(yinv[i], bdiag(km_u[i])) for i in n_units]
    u0 = [q1[i] + _dot(yinv[i], bdiag(q1[i])) for i in n_units]
    rt2 = [rt_u[i] - _dot(a_rb[i], bdiag(km2[i])) for i in n_units]
    o0 = [q2[i] - _dot(a_rb[i], bdiag(u0[i])) for i in n_units]
    kd_b = [x.astype(BF16) for x in kd_u]
    bd_b = [x.astype(BF16) for x in bd_u]
    n_c = [_diag_blocks(_dot_tn(kd_b[i], v_u[i].astype(BF16)), masks)
           - _diag_blocks(_dot_tn(bd_b[i], u0[i].astype(BF16)), masks) for i in n_units]
    m_c = [jnp.where(eye, dec[rs.stop - 1:rs.stop, sl], 0.0)
           - _diag_blocks(_dot_tn(bd_b[i], km2[i].astype(BF16)), masks) for i, (rs, sl) in enumerate(units)]
    l_split = [_split2(jnp.concatenate([rt2[i], m_c[i]], axis=0)) for i in n_units]

    h = [h_ref[:, sl] for sl in lane_groups]
    y_rows = []
    for ci in range(CHUNKS_PER_STEP):
        y_parts = []
        for grp in range(N_GROUPS):
            i = ci * N_GROUPS + grp
            h_hi, h_lo = _split2(h[grp])
            h_hi, h_lo = _block_diag(h_hi, masks), _block_diag(h_lo, masks)
            l_hi, l_lo = l_split[i]
            res = _dot(l_hi, h_hi) + _dot(l_hi, h_lo) + _dot(l_lo, h_hi)
            y_parts.append(o0[i] + res[:CHUNK])
            h[grp] = res[CHUNK:] + n_c[i]
        y_rows.append(jnp.concatenate(y_parts, axis=1))
    for grp, sl in enumerate(lane_groups):
        h_ref[:, sl] = h[grp]

    y = jnp.concatenate(y_rows, axis=0)
    o_ref[...] = _rwkv_out(y, ur, kmod, uv, g, rk_ref[...], lnw_ref[...], lnb_ref[...], head_ones)

    @pl.when(c == pl.num_programs(1) - 1)
    def _():
        wkv_ref[0] = h_ref[...].T


def _rwkv_prompt(proj, mu_r, mu_k, mu_v, mu_l, w0, a0, w_lora, k_k, k_a, r_k, ln_w, ln_b, b, t):
    nc = t // STEP_ROWS
    row = lambda c: pl.BlockSpec((STEP_ROWS, D_MODEL), lambda bi, ci: (bi * nc + ci, c))
    vec = pl.BlockSpec((1, D_MODEL), lambda bi, ci: (0, 0))
    vec_l = pl.BlockSpec((1, LORA_W), lambda bi, ci: (0, 0))
    return pl.pallas_call(
        _rwkv_prompt_kernel,
        out_shape=(
            jax.ShapeDtypeStruct((b * t, D_MODEL), F32),
            jax.ShapeDtypeStruct((b, D_MODEL, HEAD_DIM), F32),
        ),
        grid=(b, nc),
        in_specs=[
            row(COL_R), row(COL_RK), row(COL_RV),
            pl.BlockSpec((STEP_ROWS, LORA_W), lambda bi, ci: (bi * nc + ci, COL_LORA)),
            vec, vec, vec, vec_l, vec, vec,
            pl.BlockSpec((LORA_W, 3 * D_MODEL), lambda bi, ci: (0, 0)),
            vec, vec, vec, vec, vec,
        ],
        out_specs=(
            row(0),
            pl.BlockSpec((1, D_MODEL, HEAD_DIM), lambda bi, ci: (bi, 0, 0)),
        ),
        scratch_shapes=[
            pltpu.VMEM((1, D_MODEL), F32), pltpu.VMEM((1, D_MODEL), F32),
            pltpu.VMEM((1, D_MODEL), F32), pltpu.VMEM((1, LORA_W), F32),
            pltpu.VMEM((HEAD_DIM, D_MODEL), F32),
        ],
        compiler_params=_cparams(2, 40),
        name="rwkv_prompt",
    )(proj, proj, proj, proj, mu_r, mu_k, mu_v, mu_l, w0, a0, w_lora, k_k, k_a, r_k, ln_w, ln_b)


def _merge_ffn_kernel(x_ref, ga_ref, gb_ref, oa_ref, ob_ref, wo_ref, gf_ref, wup_ref, wdn_ref, y_ref):
    mix = jax.nn.sigmoid(ga_ref[...]) * oa_ref[...] + jax.nn.sigmoid(gb_ref[...]) * ob_ref[...]
    x1 = x_ref[...] + _dot(mix.astype(BF16), wo_ref[...])
    h = _rmsnorm(x1, gf_ref[...]).astype(BF16)
    acc = x1
    for c in range(D_FF // D_MODEL):
        sl = slice(c * D_MODEL, (c + 1) * D_MODEL)
        f = jnp.maximum(_dot(h, wup_ref[:, sl]), 0.0)
        acc = acc + _dot((f * f).astype(BF16), wdn_ref[sl, :])
    y_ref[...] = acc


def _merge_ffn(x, proj, o_a, o_b, w_o, g_ffn, w_up, w_down):
    m = x.shape[0]
    tm = min(256, m)
    row = lambda c: pl.BlockSpec((tm, D_MODEL), lambda i: (i, c))
    once = pl.Buffered(1)
    return pl.pallas_call(
        _merge_ffn_kernel,
        out_shape=jax.ShapeDtypeStruct((m, D_MODEL), F32),
        grid=(m // tm,),
        in_specs=[
            row(0), row(COL_GA), row(COL_GB), row(0), row(0),
            pl.BlockSpec((D_MODEL, D_MODEL), lambda i: (0, 0), pipeline_mode=once),
            pl.BlockSpec((1, D_MODEL), lambda i: (0, 0)),
            pl.BlockSpec((D_MODEL, D_FF), lambda i: (0, 0), pipeline_mode=once),
            pl.BlockSpec((D_FF, D_MODEL), lambda i: (0, 0), pipeline_mode=once),
        ],
        out_specs=row(0),
        compiler_params=_cparams(1, 48),
        name="merge_ffn",
    )(x, proj, proj, o_a, o_b, w_o, g_ffn, w_up, w_down)


def _sample_prep_kernel(p_ref, sr_ref, sk_ref, sv_ref, sl_ref, gq_ref, gk_ref,
                        mur_ref, muk_ref, muv_ref, mul_ref, w0_ref, a0_ref, wl_ref, kk_ref, ka_ref,
                        q_ref, kn_ref, r_ref, dec_ref, kmod_ref, v_ref, kkn_ref, b_ref, g_ref):
    head_ones = _head_ones()
    col = lambda c: p_ref[:, c * D_MODEL:(c + 1) * D_MODEL]
    q_ref[...] = _head_rmsnorm(col(COL_Q), gq_ref[...], head_ones)
    kn_ref[...] = _head_rmsnorm(col(COL_K), gk_ref[...], head_ones)

    def mix(x, prev, mu):
        return x + (prev - x) * mu

    ur = mix(col(COL_R), sr_ref[...], mur_ref[...])
    uk = mix(col(COL_RK), sk_ref[...], muk_ref[...])
    uv = mix(col(COL_RV), sv_ref[...], muv_ref[...])
    ul = mix(p_ref[:, COL_LORA * LORA_W:], sl_ref[...], mul_ref[...])
    e, a, g, kk, kmod = _rwkv_mix(ur, uk, ul, w0_ref[...], a0_ref[...], wl_ref[...],
                                  kk_ref[...], ka_ref[...], head_ones)
    r_ref[...] = ur
    dec_ref[...] = jnp.exp(-e)
    kmod_ref[...] = kmod
    v_ref[...] = uv
    kkn_ref[...] = kk
    b_ref[...] = kk * a
    g_ref[...] = g


def _sample_prep(proj_s, shift_r, shift_k, shift_v, shift_l, gq_t, gk_t,
                 mu_r, mu_k, mu_v, mu_l, w0, a0, w_lora, k_k, k_a):
    n = proj_s.shape[0]
    out = jax.ShapeDtypeStruct((n, D_MODEL), F32)
    return pl.pallas_call(
        _sample_prep_kernel,
        out_shape=(out,) * 9,
        compiler_params=_cparams(0, 40),
        name="sample_prep",
    )(proj_s, shift_r, shift_k, shift_v, shift_l, gq_t, gk_t,
      mu_r, mu_k, mu_v, mu_l, w0, a0, w_lora, k_k, k_a)


PAGES_PER_STEP = 8
N_SEL_PAGES = PAGES_PER_BLOCK * TOP_K
PAGE_IDS_W = 8
SUBLANES = 8


def _page_means_kernel(pt_ref, *refs):
    page_refs, o_ref = refs[:PAGES_PER_STEP], refs[PAGES_PER_STEP]
    eye = (_iota((HEAD_DIM, HEAD_DIM), 0) == _iota((HEAD_DIM, HEAD_DIM), 1)).astype(F32)[None]
    for n in range(PAGES_PER_STEP // PAGES_PER_BLOCK):
        tot = None
        for e in range(PAGES_PER_BLOCK):
            page = page_refs[n * PAGES_PER_BLOCK + e][0, 0]
            tot = page if tot is None else tot + page
        col = jnp.sum(tot, axis=-1, keepdims=True)
        o_ref[0, n] = jnp.sum(col * eye, axis=1) * (1.0 / MOBA_BLOCK)


def _page_block_means(cache_kt, page_table):
    bs, n_pages = page_table.shape

    def page_spec(e):
        return pl.BlockSpec(
            (1, 1, HEADS, HEAD_DIM, PAGE_ROWS),
            lambda bi, s, pt: (0, pt[bi * n_pages + s * PAGES_PER_STEP + e], 0, 0, 0))

    blocks_per_step = PAGES_PER_STEP // PAGES_PER_BLOCK
    return pl.pallas_call(
        _page_means_kernel,
        out_shape=jax.ShapeDtypeStruct((bs, n_pages // PAGES_PER_BLOCK, HEADS, HEAD_DIM), F32),
        grid_spec=pltpu.PrefetchScalarGridSpec(
            num_scalar_prefetch=1,
            grid=(bs, n_pages // PAGES_PER_STEP),
            in_specs=[page_spec(e) for e in range(PAGES_PER_STEP)],
            out_specs=pl.BlockSpec((1, blocks_per_step, HEADS, HEAD_DIM), lambda bi, s, pt: (bi, s, 0, 0)),
        ),
        compiler_params=_cparams(2, 32),
        name="page_block_means",
    )(page_table.reshape(-1), *([cache_kt] * PAGES_PER_STEP))


def _sample_select_kernel(km_ref, q_ref, pe_ref, po_ref, o_ref, *, nb):
    km = km_ref[0]
    q = q_ref[0]
    gate = jnp.sum(km * q[None], axis=-1, keepdims=True)
    blk = _iota((nb, 1, 1), 0)
    rank = jnp.zeros(gate.shape, F32)
    for m in range(nb):
        gm = gate[m:m + 1]
        beats = (gm > gate) | ((gm == gate) & (blk > m))
        rank = rank + beats.astype(F32)
    lane = _iota((1, PAGE_IDS_W), 1)
    out = jnp.zeros((HEADS, PAGE_IDS_W), F32)
    for s in range(TOP_K):
        slot = (rank == s).astype(F32)
        for e, p_ref in enumerate((pe_ref, po_ref)):
            page = jnp.sum(slot * p_ref[0], axis=0)
            out = out + page * (lane == PAGES_PER_BLOCK * s + e).astype(F32)
    o_ref[0] = out.astype(I32)


def _sample_select(kmean_s, q_s, pages_even, pages_odd):
    bs, nb = kmean_s.shape[:2]
    return pl.pallas_call(
        functools.partial(_sample_select_kernel, nb=nb),
        out_shape=jax.ShapeDtypeStruct((bs, HEADS, PAGE_IDS_W), I32),
        grid=(bs,),
        in_specs=[
            pl.BlockSpec((1, nb, HEADS, HEAD_DIM), lambda bi: (bi, 0, 0, 0)),
            pl.BlockSpec((1, HEADS, HEAD_DIM), lambda bi: (bi, 0, 0)),
            pl.BlockSpec((1, nb, 1, 1), lambda bi: (bi, 0, 0, 0)),
            pl.BlockSpec((1, nb, 1, 1), lambda bi: (bi, 0, 0, 0)),
        ],
        out_specs=pl.BlockSpec((1, HEADS, PAGE_IDS_W), lambda bi: (bi, 0, 0)),
        compiler_params=_cparams(1, 32),
        name="sample_select",
    )(kmean_s, q_s, pages_even, pages_odd)


def _sample_attn_kernel(pages_ref, q_ref, kn_ref, vn_ref, ck_ref, cv_ref, o_ref, kbuf, vbuf, sem):
    bi = pl.program_id(0)
    slot = lax.rem(bi, 2)

    def copies(seq, buf):
        out = []
        for h in range(HEADS):
            for j in range(N_SEL_PAGES):
                page = pages_ref[(seq * HEADS + h) * PAGE_IDS_W + j]
                out.append(pltpu.make_async_copy(ck_ref.at[0, page, h], kbuf.at[buf, h, j], sem.at[0, buf]))
                out.append(pltpu.make_async_copy(cv_ref.at[0, page, h], vbuf.at[buf, h, j], sem.at[1, buf]))
        return out

    @pl.when(bi == 0)
    def _():
        for cp in copies(bi, slot):
            cp.start()

    @pl.when(bi + 1 < pl.num_programs(0))
    def _():
        for cp in copies(bi + 1, 1 - slot):
            cp.start()

    q_b = q_ref[0].astype(BF16)
    s_new = jnp.sum(q_b.astype(F32) * kn_ref[0].astype(BF16).astype(F32), axis=-1, keepdims=True) * ATT_SCALE
    v_new = vn_ref[0].astype(BF16).astype(F32)

    for cp in copies(bi, slot):
        cp.wait()

    for h in range(HEADS):
        q8 = jnp.broadcast_to(q_b[h:h + 1, :], (SUBLANES, HEAD_DIM))
        s = jnp.concatenate(
            [_dot(q8, kbuf[slot, h, j].astype(BF16))[0:1, :] for j in range(N_SEL_PAGES)], axis=1) * ATT_SCALE
        sn = s_new[h:h + 1, :]
        m = jnp.maximum(jnp.max(s, axis=-1, keepdims=True), sn)
        p = jnp.exp(s - m)
        pn = jnp.exp(sn - m)
        denom = jnp.sum(p, axis=-1, keepdims=True) + pn
        p_b = p.astype(BF16)
        acc = pn.astype(BF16).astype(F32) * v_new[h:h + 1, :]
        for j in range(N_SEL_PAGES):
            p8 = jnp.broadcast_to(p_b[:, j * PAGE_ROWS:(j + 1) * PAGE_ROWS], (SUBLANES, PAGE_ROWS))
            acc = acc + _dot_nt(p8, vbuf[slot, h, j].astype(BF16))[0:1, :]
        o_ref[0, h:h + 1, :] = acc / denom


def _sample_attn(pages, q_s, k_new, v_new, cache_kt, cache_vt):
    bs = q_s.shape[0]
    vec = pl.BlockSpec((1, HEADS, HEAD_DIM), lambda bi, pg: (bi, 0, 0))
    return pl.pallas_call(
        _sample_attn_kernel,
        out_shape=jax.ShapeDtypeStruct((bs, HEADS, HEAD_DIM), F32),
        grid_spec=pltpu.PrefetchScalarGridSpec(
            num_scalar_prefetch=1,
            grid=(bs,),
            in_specs=[vec, vec, vec, pl.BlockSpec(memory_space=pl.ANY), pl.BlockSpec(memory_space=pl.ANY)],
            out_specs=vec,
            scratch_shapes=[
                pltpu.VMEM((2, HEADS, N_SEL_PAGES, HEAD_DIM, PAGE_ROWS), F32),
                pltpu.VMEM((2, HEADS, N_SEL_PAGES, HEAD_DIM, PAGE_ROWS), F32),
                pltpu.SemaphoreType.DMA((2, 2)),
            ],
        ),
        compiler_params=_cparams(1, 40),
        name="sample_attn",
    )(pages.reshape(-1), q_s, k_new, v_new, cache_kt, cache_vt)


def _sample_wkv_kernel(s_ref, dec_ref, kk_ref, b_ref, k_ref, v_ref, r_ref, so_ref, y_ref):
    st = s_ref[0]
    eye = (_iota((HEAD_DIM, HEAD_DIM), 0) == _iota((HEAD_DIM, HEAD_DIM), 1)).astype(F32)[None]
    sa = jnp.sum(st * kk_ref[0], axis=-1, keepdims=True)
    v_col = jnp.sum(v_ref[0] * eye, axis=-1, keepdims=True)
    s_new = st * dec_ref[0] - sa * b_ref[0] + v_col * k_ref[0]
    so_ref[0] = s_new
    y_col = jnp.sum(s_new * r_ref[0], axis=-1, keepdims=True)
    y_ref[0] = jnp.sum(y_col * eye, axis=1, keepdims=True)


def _sample_wkv(state, dec, kk, bvec, kmod, v, r):
    bs = state.shape[0]
    vec = pl.BlockSpec((1, HEADS, 1, HEAD_DIM), lambda bi: (bi, 0, 0, 0))
    st = pl.BlockSpec((1, HEADS, HEAD_DIM, HEAD_DIM), lambda bi: (bi, 0, 0, 0))
    as_vec = lambda x: x.reshape(bs, HEADS, 1, HEAD_DIM)
    return pl.pallas_call(
        _sample_wkv_kernel,
        out_shape=(
            jax.ShapeDtypeStruct(state.shape, F32),
            jax.ShapeDtypeStruct((bs, HEADS, 1, HEAD_DIM), F32),
        ),
        grid=(bs,),
        in_specs=[st] + [vec] * 6,
        out_specs=(st, vec),
        compiler_params=_cparams(1, 32),
        name="sample_wkv",
    )(state, as_vec(dec), as_vec(kk), as_vec(bvec), as_vec(kmod), as_vec(v), as_vec(r))


def _sample_rwkv_out_kernel(y_ref, r_ref, kmod_ref, v_ref, g_ref, rk_ref, lnw_ref, lnb_ref, o_ref):
    o_ref[...] = _rwkv_out(y_ref[...], r_ref[...], kmod_ref[...], v_ref[...], g_ref[...],
                           rk_ref[...], lnw_ref[...], lnb_ref[...], _head_ones())


def _sample_rwkv_out(y, r, kmod, v, g, r_k, ln_w, ln_b):
    return pl.pallas_call(
        _sample_rwkv_out_kernel,
        out_shape=jax.ShapeDtypeStruct(y.shape, F32),
        compiler_params=_cparams(0, 32),
        name="sample_rwkv_out",
    )(y, r, kmod, v, g, r_k, ln_w, ln_b)


def kernel(x_prompt, x_sample, cache_k, cache_v, state_wkv, state_shift, page_table, g_mix, w_in, g_q, g_k,
           mu_shift, w0, w_w2, a0, w_a2, w_g2, k_k, k_a, r_k, ln_x_w, ln_x_b, w_o, g_ffn, w_up, w_down):
    b, t, d = x_prompt.shape
    bs, ts, _ = x_sample.shape
    depth = w_in.shape[0]
    assert depth == 1 and ts == 1 and d == D_MODEL
    assert t % MOBA_BLOCK == 0 and t % STEP_ROWS == 0 and page_table.shape[1] % PAGES_PER_STEP == 0
    rwkv_w = 3 * D_MODEL
    off_rwkv, off_gate = 3 * D_MODEL, 6 * D_MODEL + LORA_W
    assert w_in.shape[2] == PROJ_W

    w = w_in[0]
    w_perm = jnp.concatenate(
        [w[:, :off_rwkv + rwkv_w], w[:, off_gate:], w[:, off_rwkv + rwkv_w:off_gate]], axis=1).astype(BF16)
    w_lora = jnp.zeros((LORA_W, 3 * D_MODEL), F32)
    w_lora = w_lora.at[:DECAY_LORA, :D_MODEL].set(w_w2[0])
    w_lora = w_lora.at[DECAY_LORA:DECAY_LORA + AAA_LORA, D_MODEL:2 * D_MODEL].set(w_a2[0])
    w_lora = w_lora.at[DECAY_LORA + AAA_LORA:, 2 * D_MODEL:].set(w_g2[0]).astype(BF16)
    vec = lambda x: x.reshape(1, -1).astype(F32)
    gq_t = vec(jnp.tile(g_q[0], HEADS))
    gk_t = vec(jnp.tile(g_k[0], HEADS))
    mu = mu_shift[0]
    mu_r, mu_k, mu_v, mu_l = (vec(mu[:D_MODEL]), vec(mu[D_MODEL:2 * D_MODEL]),
                              vec(mu[2 * D_MODEL:rwkv_w]), vec(mu[rwkv_w:]))
    rw = (vec(w0[0]), vec(a0[0]), w_lora, vec(k_k[0]), vec(k_a[0]))
    r_k_v, ln_w, ln_b = vec(r_k[0]), vec(ln_x_w[0]), vec(ln_x_b[0])
    g_mix_v, g_ffn_v = vec(g_mix[0]), vec(g_ffn[0])
    w_o_b, w_up_b, w_down_b = w_o[0].astype(BF16), w_up[0].astype(BF16), w_down[0].astype(BF16)

    def shift_rows(p_rows):
        return jnp.concatenate([p_rows[:, COL_R * D_MODEL:(COL_RV + 1) * D_MODEL],
                                p_rows[:, COL_LORA * LORA_W:]], axis=-1)

    xp = x_prompt.reshape(b * t, d)
    proj_p = _project(xp, g_mix_v, w_perm)
    k_p, v_p, q_pair, k_pair, vt, kmean = _attn_prep(proj_p, gq_t, gk_t, b, t)
    o_att = _moba_prompt(q_pair, k_pair, vt, kmean.reshape(b, t // MOBA_BLOCK, D_MODEL))
    o_rwkv, wkv_t = _rwkv_prompt(proj_p, mu_r, mu_k, mu_v, mu_l, *rw, r_k_v, ln_w, ln_b, b, t)
    y_p = _merge_ffn(xp, proj_p, o_att.reshape(b * t, d), o_rwkv, w_o_b, g_ffn_v, w_up_b, w_down_b)
    shift_p = shift_rows(proj_p.reshape(b, t, PROJ_W)[:, -1])

    xs = x_sample.reshape(bs, d)
    proj_s = _project(xs, g_mix_v, w_perm)
    sh = state_shift[0]
    (q_s, kn_s, r_s, dec_s, kmod_s, v_s, kk_s, b_s, g_s) = _sample_prep(
        proj_s, sh[:, :D_MODEL], sh[:, D_MODEL:2 * D_MODEL], sh[:, 2 * D_MODEL:rwkv_w], sh[:, rwkv_w:],
        gq_t, gk_t, mu_r, mu_k, mu_v, mu_l, *rw)
    v_new = proj_s[:, COL_V * D_MODEL:(COL_V + 1) * D_MODEL]
    heads = lambda x: x.reshape(bs, HEADS, HEAD_DIM)
    cache_kt = cache_k.transpose(0, 1, 3, 4, 2)
    cache_vt = cache_v.transpose(0, 1, 3, 4, 2)
    kmean_s = _page_block_means(cache_kt, page_table)
    n_blk = page_table.shape[1] // PAGES_PER_BLOCK
    pt_f = page_table.astype(F32).reshape(bs, n_blk, PAGES_PER_BLOCK)
    pages = _sample_select(kmean_s, heads(q_s), pt_f[:, :, 0].reshape(bs, n_blk, 1, 1),
                           pt_f[:, :, 1].reshape(bs, n_blk, 1, 1))
    o_att_s = _sample_attn(pages, heads(q_s), heads(kn_s), heads(v_new), cache_kt, cache_vt)
    wkv_s, y_s = _sample_wkv(state_wkv[0], dec_s, kk_s, b_s, kmod_s, v_s, r_s)
    o_rwkv_s = _sample_rwkv_out(y_s.reshape(bs, d), r_s, kmod_s, v_s, g_s, r_k_v, ln_w, ln_b)
    y_samp = _merge_ffn(xs, proj_s, o_att_s.reshape(bs, d), o_rwkv_s, w_o_b, g_ffn_v, w_up_b, w_down_b)

    hd = (HEADS, HEAD_DIM)
    return (
        y_p.reshape(b, t, d),
        y_samp.reshape(bs, ts, d),
        k_p.reshape(1, b, *hd, t).transpose(0, 1, 4, 2, 3),
        v_p.reshape(1, b, *hd, t).transpose(0, 1, 4, 2, 3),
        kn_s.reshape(1, bs, ts, *hd),
        v_new.reshape(1, bs, ts, *hd),
        wkv_t.reshape(1, b, HEADS, HEAD_DIM, HEAD_DIM),
        wkv_s[None],
        shift_p[None],
        shift_rows(proj_s)[None],
    )
```

```python
import functools

import jax
import jax.numpy as jnp
from jax import lax
from jax.experimental import pallas as pl
from jax.experimental.pallas import tpu as pltpu

F32 = jnp.float32
BF16 = jnp.bfloat16
I32 = jnp.int32

D_MODEL = 1024
HEADS = 16
HEAD_DIM = 64
MOBA_BLOCK = 256
TOP_K = 3
PAGE_ROWS = 128
PAGES_PER_BLOCK = MOBA_BLOCK // PAGE_ROWS
ATT_SCALE = HEAD_DIM ** -0.5
DECAY_LORA = 64
AAA_LORA = 64
GATE_LORA = 128
LORA_W = DECAY_LORA + AAA_LORA + GATE_LORA
D_FF = 4 * D_MODEL
NORM_EPS = 1e-6
GN_EPS = 6.4e-4
EXP_M05 = 0.6065306597126334
MASK_NEG = -1e30

COL_Q, COL_K, COL_V, COL_R, COL_RK, COL_RV, COL_GA, COL_GB = range(8)
COL_LORA = 8 * D_MODEL // LORA_W
PROJ_W = 8 * D_MODEL + LORA_W

MXU_LANES = 256
VREG_LANES = 128
HEADS_PER_GROUP = MXU_LANES // HEAD_DIM
N_GROUPS = D_MODEL // MXU_LANES
HEAD_PAIRS = D_MODEL // VREG_LANES
CHUNK = 64
assert CHUNK == HEAD_DIM
CHUNKS_PER_STEP = 2
STEP_ROWS = CHUNKS_PER_STEP * CHUNK
VMEM_LIMIT_CAP = 56 * 1024 * 1024


def _cparams(n_axes, vmem_mib):
    return pltpu.CompilerParams(
        dimension_semantics=("arbitrary",) * n_axes,
        vmem_limit_bytes=min(vmem_mib * 1024 * 1024, VMEM_LIMIT_CAP),
    )


def _dot(a, b):
    return jnp.dot(a, b, preferred_element_type=F32)


def _dot_nt(a, b):
    return lax.dot_general(a, b, (((1,), (1,)), ((), ())), preferred_element_type=F32)


def _dot_tn(a, b):
    return lax.dot_general(a, b, (((0,), (0,)), ((), ())), preferred_element_type=F32)


def _split2(x):
    hi = x.astype(BF16)
    lo = (x - hi.astype(F32)).astype(BF16)
    return hi, lo


def _split3(x):
    h1 = x.astype(BF16)
    r1 = x - h1.astype(F32)
    h2 = r1.astype(BF16)
    h3 = (r1 - h2.astype(F32)).astype(BF16)
    return h1, h2, h3


def _iota(shape, axis):
    return lax.broadcasted_iota(I32, shape, axis)


def _head_ones():
    r = lax.shift_right_logical(_iota((MXU_LANES, MXU_LANES), 0), 6)
    c = lax.shift_right_logical(_iota((MXU_LANES, MXU_LANES), 1), 6)
    return (r == c).astype(BF16)


def _head_sum(x, head_ones):
    hi, lo = _split2(x)
    parts = []
    for g in range(N_GROUPS):
        sl = slice(g * MXU_LANES, (g + 1) * MXU_LANES)
        parts.append(_dot(hi[:, sl], head_ones) + _dot(lo[:, sl], head_ones))
    return jnp.concatenate(parts, axis=1)


def _rmsnorm(x, g):
    ms = jnp.mean(x * x, axis=-1, keepdims=True)
    return x * lax.rsqrt(ms + NORM_EPS) * g


def _head_rmsnorm(x, g, head_ones):
    ss = _head_sum(x * x, head_ones)
    return x * lax.rsqrt(ss * (1.0 / HEAD_DIM) + NORM_EPS) * g


def _rwkv_mix(ur, uk, ul, w0, a0, w_lora, k_k, k_a, head_ones):
    lane = _iota((1, LORA_W), 1)
    t = jnp.where(lane < DECAY_LORA, jnp.tanh(ul),
                  jnp.where(lane < DECAY_LORA + AAA_LORA, ul, jax.nn.sigmoid(ul)))
    lo = _dot(t.astype(BF16), w_lora)
    e = EXP_M05 * jax.nn.sigmoid(w0 + lo[:, :D_MODEL])
    a = jax.nn.sigmoid(a0 + lo[:, D_MODEL:2 * D_MODEL])
    g = lo[:, 2 * D_MODEL:]
    kk = uk * k_k
    kk = kk * lax.rsqrt(jnp.maximum(_head_sum(kk * kk, head_ones), 1e-24))
    kmod = uk * (1.0 + (a - 1.0) * k_a)
    return e, a, g, kk, kmod


def _rwkv_out(y, ur, kmod, uv, g, r_k, ln_w, ln_b, head_ones):
    inv = 1.0 / HEAD_DIM
    mean = _head_sum(y, head_ones) * inv
    d = y - mean
    var = _head_sum(d * d, head_ones) * inv
    yn = d * lax.rsqrt(var + GN_EPS) * ln_w + ln_b
    bonus = _head_sum(ur * kmod * r_k, head_ones) * uv
    return (yn + bonus) * g


def _proj_kernel(x_ref, g_ref, w_ref, o_ref, h_ref):
    @pl.when(pl.program_id(1) == 0)
    def _():
        h_ref[...] = _rmsnorm(x_ref[...], g_ref[...]).astype(BF16)

    o_ref[...] = _dot(h_ref[...], w_ref[...])


def _project(x, g_mix, w_perm):
    m = x.shape[0]
    tm = min(1024, m)
    tn = PROJ_W // 6
    return pl.pallas_call(
        _proj_kernel,
        out_shape=jax.ShapeDtypeStruct((m, PROJ_W), F32),
        grid=(m // tm, PROJ_W // tn),
        in_specs=[
            pl.BlockSpec((tm, D_MODEL), lambda i, j: (i, 0)),
            pl.BlockSpec((1, D_MODEL), lambda i, j: (0, 0)),
            pl.BlockSpec((D_MODEL, tn), lambda i, j: (0, j)),
        ],
        out_specs=pl.BlockSpec((tm, tn), lambda i, j: (i, j)),
        scratch_shapes=[pltpu.VMEM((tm, D_MODEL), BF16)],
        compiler_params=_cparams(2, 40),
        name="proj",
    )(x, g_mix, w_perm)


def _attn_prep_kernel(q_ref, k_ref, v_ref, gq_ref, gk_ref,
                      kout_ref, vout_ref, qp_ref, kp_ref, vt_ref, kmean_ref):
    head_ones = _head_ones()
    qn = _head_rmsnorm(q_ref[...], gq_ref[...], head_ones) * ATT_SCALE
    kn = _head_rmsnorm(k_ref[...], gk_ref[...], head_ones)
    vt_f = v_ref[...].T
    kout_ref[0] = kn.T
    vout_ref[0] = vt_f
    for p in range(HEAD_PAIRS):
        sl = slice(p * VREG_LANES, (p + 1) * VREG_LANES)
        qp_ref[0, p] = qn[:, sl].astype(BF16)
        kp_ref[0, p] = kn[:, sl].astype(BF16)
    vt = vt_f.astype(BF16)
    for p in range(HEAD_PAIRS):
        vt_ref[0, p, 0] = vt[p * VREG_LANES:(p + 1) * VREG_LANES, :]
    kmean_ref[0, 0] = jnp.mean(kn, axis=0, keepdims=True)


def _attn_prep(proj, gq_t, gk_t, b, t):
    nb = t // MOBA_BLOCK
    row = lambda c: pl.BlockSpec((MOBA_BLOCK, D_MODEL), lambda bi, j: (bi * nb + j, c))
    vec = pl.BlockSpec((1, D_MODEL), lambda bi, j: (0, 0))
    pair = pl.BlockSpec((1, HEAD_PAIRS, MOBA_BLOCK, VREG_LANES), lambda bi, j: (bi, 0, j, 0))
    chan_major = pl.BlockSpec((1, D_MODEL, MOBA_BLOCK), lambda bi, j: (bi, 0, j))
    return pl.pallas_call(
        _attn_prep_kernel,
        out_shape=(
            jax.ShapeDtypeStruct((b, D_MODEL, t), F32),
            jax.ShapeDtypeStruct((b, D_MODEL, t), F32),
            jax.ShapeDtypeStruct((b, HEAD_PAIRS, t, VREG_LANES), BF16),
            jax.ShapeDtypeStruct((b, HEAD_PAIRS, t, VREG_LANES), BF16),
            jax.ShapeDtypeStruct((b, HEAD_PAIRS, nb, VREG_LANES, MOBA_BLOCK), BF16),
            jax.ShapeDtypeStruct((b, nb, 1, D_MODEL), F32),
        ),
        grid=(b, nb),
        in_specs=[row(COL_Q), row(COL_K), row(COL_V), vec, vec],
        out_specs=(
            chan_major, chan_major, pair, pair,
            pl.BlockSpec((1, HEAD_PAIRS, 1, VREG_LANES, MOBA_BLOCK), lambda bi, j: (bi, 0, j, 0, 0)),
            pl.BlockSpec((1, 1, 1, D_MODEL), lambda bi, j: (bi, j, 0, 0)),
        ),
        compiler_params=_cparams(2, 40),
        name="attn_prep",
    )(proj, proj, proj, gq_t, gk_t)


def _moba_kernel(q_ref, k_ref, vt_ref, km_ref, o_ref, sel_ref, p_ref, s_ref, *, nb):
    i = pl.program_id(2)
    q = q_ref[0, 0]
    km = km_ref[0]
    lane = _iota((1, VREG_LANES), 1)
    blk = _iota((nb, 1), 0)
    causal = _iota((MOBA_BLOCK, MOBA_BLOCK), 0) <= _iota((MOBA_BLOCK, MOBA_BLOCK), 1)
    valid = blk < i
    head_slices = [slice(hh * HEAD_DIM, (hh + 1) * HEAD_DIM) for hh in range(2)]
    heads = range(2)
    in_head = [(lane >= hh * HEAD_DIM) & (lane < (hh + 1) * HEAD_DIM) for hh in heads]
    qms = [jnp.where(in_head[hh], q, jnp.zeros_like(q)) for hh in heads]

    k_i = k_ref[0, 0, pl.ds(pl.multiple_of(i * MOBA_BLOCK, MOBA_BLOCK), MOBA_BLOCK), :]
    s_own = [jnp.where(causal, _dot_nt(k_i, qms[hh]), MASK_NEG) for hh in heads]

    km_split = [_split2(jnp.where(in_head[hh], km, 0.0)) for hh in heads]
    gate = [jnp.where(valid, _dot_nt(km_split[hh][0], qms[hh]) + _dot_nt(km_split[hh][1], qms[hh]), -jnp.inf)
            for hh in heads]
    rank = [jnp.zeros(gate[0].shape, F32) for _ in heads]
    for m in range(nb):
        for hh in heads:
            gm = gate[hh][m:m + 1, :]
            beats = (gm > gate[hh]) | ((gm == gate[hh]) & (blk > m))
            rank[hh] = rank[hh] + beats.astype(F32)
    for hh in heads:
        sel_ref[hh] = (valid & (rank[hh] < TOP_K)).astype(F32)

    m0 = [jnp.max(s_own[hh], axis=0, keepdims=True) for hh in heads]
    p_own = [jnp.exp(s_own[hh] - m0[hh]) for hh in heads]
    l0 = [jnp.sum(p_own[hh], axis=0, keepdims=True) for hh in heads]
    for hh in heads:
        p_ref[hh, :MOBA_BLOCK] = p_own[hh].astype(BF16)
        p_ref[hh, MOBA_BLOCK:] = jnp.zeros((MOBA_BLOCK, MOBA_BLOCK), BF16)
    init = []
    for hh in heads:
        init += [m0[hh], l0[hh], jnp.zeros((HEAD_DIM, MOBA_BLOCK), F32)]

    def pending_pv(hh, jp0, jp1):
        return (_dot(vt_ref[0, 0, jp0, head_slices[hh], :], p_ref[hh, :MOBA_BLOCK])
                + _dot(vt_ref[0, 0, jp1, head_slices[hh], :], p_ref[hh, MOBA_BLOCK:]))

    def scores(j_first, hh):
        rows = pl.ds(pl.multiple_of(j_first * MOBA_BLOCK, MOBA_BLOCK), 2 * MOBA_BLOCK)
        return _dot_nt(k_ref[0, 0, rows, :], qms[hh])

    for hh in heads:
        s_ref[0, hh] = scores(0, hh)

    def body(jj, carry):
        j0 = 2 * jj
        j1 = j0 + 1
        jp0, jp1 = carry[6], carry[7]
        cur = jj & 1
        j_next = jnp.minimum(j0 + 2, nb - 2)
        heads = range(2)
        pv = [pending_pv(hh, jp0, jp1) for hh in heads]
        s = [s_ref[cur, hh] for hh in heads]
        for hh in heads:
            s_ref[1 - cur, hh] = scores(j_next, hh)
        pick0 = [sel_ref[hh, pl.ds(j0, 1), :] > 0.0 for hh in heads]
        pick1 = [(sel_ref[hh, pl.ds(j1, 1), :] > 0.0) & (j1 < i) for hh in heads]
        s0 = [jnp.where(pick0[hh], s[hh][:MOBA_BLOCK], MASK_NEG) for hh in heads]
        s1 = [jnp.where(pick1[hh], s[hh][MOBA_BLOCK:], MASK_NEG) for hh in heads]
        m_new = [jnp.maximum(carry[3 * hh], jnp.maximum(jnp.max(s0[hh], axis=0, keepdims=True),
                                                        jnp.max(s1[hh], axis=0, keepdims=True))) for hh in heads]
        alpha = [jnp.exp(carry[3 * hh] - m_new[hh]) for hh in heads]
        p0 = [jnp.exp(s0[hh] - m_new[hh]) for hh in heads]
        p1 = [jnp.exp(s1[hh] - m_new[hh]) for hh in heads]
        l_new = [alpha[hh] * carry[3 * hh + 1] + jnp.sum(p0[hh], axis=0, keepdims=True)
                 + jnp.sum(p1[hh], axis=0, keepdims=True) for hh in heads]
        out = []
        for hh in heads:
            out += [m_new[hh], l_new[hh], alpha[hh] * (carry[3 * hh + 2] + pv[hh])]
            p_ref[hh, :MOBA_BLOCK] = p0[hh].astype(BF16)
            p_ref[hh, MOBA_BLOCK:] = p1[hh].astype(BF16)
        return tuple(out) + (j0, j1)

    fin = lax.fori_loop(0, lax.shift_right_logical(i + 1, 1), body, tuple(init) + (i, i))
    outs = [(fin[3 * hh + 2] + pending_pv(hh, fin[6], fin[7])) / fin[3 * hh + 1] for hh in heads]
    o_ref[0] = jnp.concatenate(outs, axis=0).T


def _moba_prompt(q_pair, k_pair, vt, kmean):
    b, _, t, _ = q_pair.shape
    nb = t // MOBA_BLOCK
    assert nb >= 2
    return pl.pallas_call(
        functools.partial(_moba_kernel, nb=nb),
        out_shape=jax.ShapeDtypeStruct((b, t, D_MODEL), F32),
        grid=(b, HEAD_PAIRS, nb),
        in_specs=[
            pl.BlockSpec((1, 1, MOBA_BLOCK, VREG_LANES), lambda bi, p, i: (bi, p, i, 0)),
            pl.BlockSpec((1, 1, t, VREG_LANES), lambda bi, p, i: (bi, p, 0, 0)),
            pl.BlockSpec((1, 1, nb, VREG_LANES, MOBA_BLOCK), lambda bi, p, i: (bi, p, 0, 0, 0)),
            pl.BlockSpec((1, nb, VREG_LANES), lambda bi, p, i: (bi, 0, p)),
        ],
        out_specs=pl.BlockSpec((1, MOBA_BLOCK, VREG_LANES), lambda bi, p, i: (bi, i, p)),
        scratch_shapes=[pltpu.VMEM((2, nb, MOBA_BLOCK), F32),
                        pltpu.VMEM((2, 2 * MOBA_BLOCK, MOBA_BLOCK), BF16),
                        pltpu.VMEM((2, 2, 2 * MOBA_BLOCK, MOBA_BLOCK), F32)],
        compiler_params=_cparams(3, 32),
        name="moba_prompt",
    )(q_pair, k_pair, vt, kmean)


def _group_masks():
    lane = _iota((1, MXU_LANES), 1)
    head = lax.shift_right_logical(lane, 6)
    return [head == h for h in range(HEADS_PER_GROUP)]


def _block_diag(x, masks):
    zero = jnp.zeros_like(x)
    return jnp.concatenate([jnp.where(mk, x, zero) for mk in masks], axis=0)


def _diag_blocks(g, masks):
    out = None
    for h, mk in enumerate(masks):
        part = jnp.where(mk, g[h * HEAD_DIM:(h + 1) * HEAD_DIM, :], 0.0)
        out = part if out is None else out + part
    return out


def _rwkv_prompt_kernel(pr_ref, pk_ref, pv_ref, pl_ref, mur_ref, muk_ref, muv_ref, mul_ref,
                        w0_ref, a0_ref, wl_ref, kk_ref, ka_ref, rk_ref, lnw_ref, lnb_ref,
                        o_ref, wkv_ref, prev_r, prev_k, prev_v, prev_l, h_ref):
    c = pl.program_id(1)

    @pl.when(c == 0)
    def _():
        prev_r[...] = jnp.zeros_like(prev_r)
        prev_k[...] = jnp.zeros_like(prev_k)
        prev_v[...] = jnp.zeros_like(prev_v)
        prev_l[...] = jnp.zeros_like(prev_l)
        h_ref[...] = jnp.zeros_like(h_ref)

    first_row = _iota((STEP_ROWS, 1), 0) == 0

    def shift(x_ref, prev_ref, mu_ref):
        x = x_ref[...]
        prev = jnp.where(first_row, prev_ref[...], pltpu.roll(x, 1, 0))
        prev_ref[...] = x[STEP_ROWS - 1:STEP_ROWS, :]
        return x + (prev - x) * mu_ref[...]

    ur = shift(pr_ref, prev_r, mur_ref)
    uk = shift(pk_ref, prev_k, muk_ref)
    uv = shift(pv_ref, prev_v, muv_ref)
    ul = shift(pl_ref, prev_l, mul_ref)

    head_ones = _head_ones()
    e, a, g, kk, kmod = _rwkv_mix(ur, uk, ul, w0_ref[...], a0_ref[...], wl_ref[...],
                                  kk_ref[...], ka_ref[...], head_ones)
    bvec = kk * a

    t_row, t_col = _iota((STEP_ROWS, STEP_ROWS), 0), _iota((STEP_ROWS, STEP_ROWS), 1)
    same_chunk = lax.shift_right_logical(t_row, 6) == lax.shift_right_logical(t_col, 6)
    tri = ((t_row >= t_col) & same_chunk).astype(BF16)
    e1, e2, e3 = _split3(e)
    cum = _dot(tri, e1) + _dot(tri, e2) + _dot(tri, e3)
    chunk_rows = [slice(ci * CHUNK, (ci + 1) * CHUNK) for ci in range(CHUNKS_PER_STEP)]
    cum_end = jnp.concatenate(
        [jnp.broadcast_to(cum[rs.stop - 1:rs.stop, :], (CHUNK, D_MODEL)) for rs in chunk_rows], axis=0)
    dec = jnp.exp(-cum)
    grow = jnp.exp(cum)
    rest = jnp.exp(cum - cum_end)
    rt = ur * dec
    km = kk * jnp.exp(e - cum)
    kp = kmod * grow
    bp = bvec * grow
    kd = kmod * rest
    bd = bvec * rest

    masks = _group_masks()
    tok = _iota((CHUNK, MXU_LANES), 0)
    col = _iota((CHUNK, MXU_LANES), 1) & (HEAD_DIM - 1)
    strict = tok > col
    incl = tok >= col
    eye = tok == col
    bdiag = lambda x: _block_diag(x.astype(BF16), masks)

    lane_groups = [slice(g * MXU_LANES, (g + 1) * MXU_LANES) for g in range(N_GROUPS)]
    units = [(rs, sl) for rs in chunk_rows for sl in lane_groups]
    n_units = range(len(units))
    cut = lambda x: [x[rs, sl] for rs, sl in units]
    km_u, rt_u, v_u, kd_u, bd_u = cut(km), cut(rt), cut(uv), cut(kd), cut(bd)
    kp_bd = [bdiag(x) for x in cut(kp)]
    bp_bd = [bdiag(x) for x in cut(bp)]
    lhs = [jnp.concatenate([km_u[i], rt_u[i]], axis=0).astype(BF16) for i in n_units]
    s1 = [_dot_nt(lhs[i], kp_bd[i]) for i in n_units]
    s2 = [_dot_nt(lhs[i], bp_bd[i]) for i in n_units]
    a_kk = [jnp.where(strict, s[:CHUNK], 0.0) for s in s1]
    a_rk = [jnp.where(incl, s[CHUNK:], 0.0) for s in s1]
    a_kb = [jnp.where(strict, s[:CHUNK], 0.0) for s in s2]
    a_rb = [jnp.where(incl, s[CHUNK:], 0.0).astype(BF16) for s in s2]

    pw = a_kb
    yinv = [-x for x in a_kb]
    pw_bd = [bdiag(x) for x in pw]
    for _ in range(5):
        pw = [_dot(pw[i].astype(BF16), pw_bd[i]) for i in n_units]
        pw_bd = [bdiag(x) for x in pw]
        yinv = [yinv[i] + pw[i] + _dot(yinv[i].astype(BF16), pw_bd[i]) for i in n_units]
    yinv = [x.astype(BF16) for x in yinv]

    v_bd = [bdiag(x) for x in v_u]
    av = [_dot(jnp.concatenate([a_kk[i], a_rk[i]], axis=0).astype(BF16), v_bd[i]) for i in n_units]
    q1 = [x[:CHUNK] for x in av]
    q2 = [x[CHUNK:] for x in av]
    km2 = [km_u[i] + _dot(yinv[i], bdiag(km_u[i])) for i in n_units]
    u0 = [q1[i] + _dot(yinv[i], bdiag(q1[i])) for i in n_units]
    rt2 = [rt_u[i] - _dot(a_rb[i], bdiag(km2[i])) for i in n_units]
    o0 = [q2[i] - _dot(a_rb[i], bdiag(u0[i])) for i in n_units]
    kd_b = [x.astype(BF16) for x in kd_u]
    bd_b = [x.astype(BF16) for x in bd_u]
    n_c = [_diag_blocks(_dot_tn(kd_b[i], v_u[i].astype(BF16)), masks)
           - _diag_blocks(_dot_tn(bd_b[i], u0[i].astype(BF16)), masks) for i in n_units]
    m_c = [jnp.where(eye, dec[rs.stop - 1:rs.stop, sl], 0.0)
           - _diag_blocks(_dot_tn(bd_b[i], km2[i].astype(BF16)), masks) for i, (rs, sl) in enumerate(units)]
    l_split = [_split2(jnp.concatenate([rt2[i], m_c[i]], axis=0)) for i in n_units]

    h = [h_ref[:, sl] for sl in lane_groups]
    y_rows = []
    for ci in range(CHUNKS_PER_STEP):
        y_parts = []
        for grp in range(N_GROUPS):
            i = ci * N_GROUPS + grp
            h_hi, h_lo = _split2(h[grp])
            h_hi, h_lo = _block_diag(h_hi, masks), _block_diag(h_lo, masks)
            l_hi, l_lo = l_split[i]
            res = _dot(l_hi, h_hi) + _dot(l_hi, h_lo) + _dot(l_lo, h_hi)
            y_parts.append(o0[i] + res[:CHUNK])
            h[grp] = res[CHUNK:] + n_c[i]
        y_rows.append(jnp.concatenate(y_parts, axis=1))
    for grp, sl in enumerate(lane_groups):
        h_ref[:, sl] = h[grp]

    y = jnp.concatenate(y_rows, axis=0)
    o_ref[...] = _rwkv_out(y, ur, kmod, uv, g, rk_ref[...], lnw_ref[...], lnb_ref[...], head_ones)

    @pl.when(c == pl.num_programs(1) - 1)
    def _():
        wkv_ref[0] = h_ref[...].T


def _rwkv_prompt(proj, mu_r, mu_k, mu_v, mu_l, w0, a0, w_lora, k_k, k_a, r_k, ln_w, ln_b, b, t):
    nc = t // STEP_ROWS
    row = lambda c: pl.BlockSpec((STEP_ROWS, D_MODEL), lambda bi, ci: (bi * nc + ci, c))
    vec = pl.BlockSpec((1, D_MODEL), lambda bi, ci: (0, 0))
    vec_l = pl.BlockSpec((1, LORA_W), lambda bi, ci: (0, 0))
    return pl.pallas_call(
        _rwkv_prompt_kernel,
        out_shape=(
            jax.ShapeDtypeStruct((b * t, D_MODEL), F32),
            jax.ShapeDtypeStruct((b, D_MODEL, HEAD_DIM), F32),
        ),
        grid=(b, nc),
        in_specs=[
            row(COL_R), row(COL_RK), row(COL_RV),
            pl.BlockSpec((STEP_ROWS, LORA_W), lambda bi, ci: (bi * nc + ci, COL_LORA)),
            vec, vec, vec, vec_l, vec, vec,
            pl.BlockSpec((LORA_W, 3 * D_MODEL), lambda bi, ci: (0, 0)),
            vec, vec, vec, vec, vec,
        ],
        out_specs=(
            row(0),
            pl.BlockSpec((1, D_MODEL, HEAD_DIM), lambda bi, ci: (bi, 0, 0)),
        ),
        scratch_shapes=[
            pltpu.VMEM((1, D_MODEL), F32), pltpu.VMEM((1, D_MODEL), F32),
            pltpu.VMEM((1, D_MODEL), F32), pltpu.VMEM((1, LORA_W), F32),
            pltpu.VMEM((HEAD_DIM, D_MODEL), F32),
        ],
        compiler_params=_cparams(2, 40),
        name="rwkv_prompt",
    )(proj, proj, proj, proj, mu_r, mu_k, mu_v, mu_l, w0, a0, w_lora, k_k, k_a, r_k, ln_w, ln_b)


def _merge_ffn_kernel(x_ref, ga_ref, gb_ref, oa_ref, ob_ref, wo_ref, gf_ref, wup_ref, wdn_ref, y_ref):
    mix = jax.nn.sigmoid(ga_ref[...]) * oa_ref[...] + jax.nn.sigmoid(gb_ref[...]) * ob_ref[...]
    x1 = x_ref[...] + _dot(mix.astype(BF16), wo_ref[...])
    h = _rmsnorm(x1, gf_ref[...]).astype(BF16)
    acc = x1
    for c in range(D_FF // D_MODEL):
        sl = slice(c * D_MODEL, (c + 1) * D_MODEL)
        f = jnp.maximum(_dot(h, wup_ref[:, sl]), 0.0)
        acc = acc + _dot((f * f).astype(BF16), wdn_ref[sl, :])
    y_ref[...] = acc


def _merge_ffn(x, proj, o_a, o_b, w_o, g_ffn, w_up, w_down):
    m = x.shape[0]
    tm = min(256, m)
    row = lambda c: pl.BlockSpec((tm, D_MODEL), lambda i: (i, c))
    once = pl.Buffered(1)
    return pl.pallas_call(
        _merge_ffn_kernel,
        out_shape=jax.ShapeDtypeStruct((m, D_MODEL), F32),
        grid=(m // tm,),
        in_specs=[
            row(0), row(COL_GA), row(COL_GB), row(0), row(0),
            pl.BlockSpec((D_MODEL, D_MODEL), lambda i: (0, 0), pipeline_mode=once),
            pl.BlockSpec((1, D_MODEL), lambda i: (0, 0)),
            pl.BlockSpec((D_MODEL, D_FF), lambda i: (0, 0), pipeline_mode=once),
            pl.BlockSpec((D_FF, D_MODEL), lambda i: (0, 0), pipeline_mode=once),
        ],
        out_specs=row(0),
        compiler_params=_cparams(1, 48),
        name="merge_ffn",
    )(x, proj, proj, o_a, o_b, w_o, g_ffn, w_up, w_down)


def _sample_prep_kernel(p_ref, sr_ref, sk_ref, sv_ref, sl_ref, gq_ref, gk_ref,
                        mur_ref, muk_ref, muv_ref, mul_ref, w0_ref, a0_ref, wl_ref, kk_ref, ka_ref,
                        q_ref, kn_ref, r_ref, dec_ref, kmod_ref, v_ref, kkn_ref, b_ref, g_ref):
    head_ones = _head_ones()
    col = lambda c: p_ref[:, c * D_MODEL:(c + 1) * D_MODEL]
    q_ref[...] = _head_rmsnorm(col(COL_Q), gq_ref[...], head_ones)
    kn_ref[...] = _head_rmsnorm(col(COL_K), gk_ref[...], head_ones)

    def mix(x, prev, mu):
        return x + (prev - x) * mu

    ur = mix(col(COL_R), sr_ref[...], mur_ref[...])
    uk = mix(col(COL_RK), sk_ref[...], muk_ref[...])
    uv = mix(col(COL_RV), sv_ref[...], muv_ref[...])
    ul = mix(p_ref[:, COL_LORA * LORA_W:], sl_ref[...], mul_ref[...])
    e, a, g, kk, kmod = _rwkv_mix(ur, uk, ul, w0_ref[...], a0_ref[...], wl_ref[...],
                                  kk_ref[...], ka_ref[...], head_ones)
    r_ref[...] = ur
    dec_ref[...] = jnp.exp(-e)
    kmod_ref[...] = kmod
    v_ref[...] = uv
    kkn_ref[...] = kk
    b_ref[...] = kk * a
    g_ref[...] = g


def _sample_prep(proj_s, shift_r, shift_k, shift_v, shift_l, gq_t, gk_t,
                 mu_r, mu_k, mu_v, mu_l, w0, a0, w_lora, k_k, k_a):
    n = proj_s.shape[0]
    out = jax.ShapeDtypeStruct((n, D_MODEL), F32)
    return pl.pallas_call(
        _sample_prep_kernel,
        out_shape=(out,) * 9,
        compiler_params=_cparams(0, 40),
        name="sample_prep",
    )(proj_s, shift_r, shift_k, shift_v, shift_l, gq_t, gk_t,
      mu_r, mu_k, mu_v, mu_l, w0, a0, w_lora, k_k, k_a)


PAGES_PER_STEP = 8
N_SEL_PAGES = PAGES_PER_BLOCK * TOP_K
PAGE_IDS_W = 8
SUBLANES = 8


def _page_means_kernel(pt_ref, *refs):
    page_refs, o_ref = refs[:PAGES_PER_STEP], refs[PAGES_PER_STEP]
    eye = (_iota((HEAD_DIM, HEAD_DIM), 0) == _iota((HEAD_DIM, HEAD_DIM), 1)).astype(F32)[None]
    for n in range(PAGES_PER_STEP // PAGES_PER_BLOCK):
        tot = None
        for e in range(PAGES_PER_BLOCK):
            page = page_refs[n * PAGES_PER_BLOCK + e][0, 0]
            tot = page if tot is None else tot + page
        col = jnp.sum(tot, axis=-1, keepdims=True)
        o_ref[0, n] = jnp.sum(col * eye, axis=1) * (1.0 / MOBA_BLOCK)


def _page_block_means(cache_kt, page_table):
    bs, n_pages = page_table.shape

    def page_spec(e):
        return pl.BlockSpec(
            (1, 1, HEADS, HEAD_DIM, PAGE_ROWS),
            lambda bi, s, pt: (0, pt[bi * n_pages + s * PAGES_PER_STEP + e], 0, 0, 0))

    blocks_per_step = PAGES_PER_STEP // PAGES_PER_BLOCK
    return pl.pallas_call(
        _page_means_kernel,
        out_shape=jax.ShapeDtypeStruct((bs, n_pages // PAGES_PER_BLOCK, HEADS, HEAD_DIM), F32),
        grid_spec=pltpu.PrefetchScalarGridSpec(
            num_scalar_prefetch=1,
            grid=(bs, n_pages // PAGES_PER_STEP),
            in_specs=[page_spec(e) for e in range(PAGES_PER_STEP)],
            out_specs=pl.BlockSpec((1, blocks_per_step, HEADS, HEAD_DIM), lambda bi, s, pt: (bi, s, 0, 0)),
        ),
        compiler_params=_cparams(2, 32),
        name="page_block_means",
    )(page_table.reshape(-1), *([cache_kt] * PAGES_PER_STEP))


def _sample_select_kernel(km_ref, q_ref, pe_ref, po_ref, o_ref, *, nb):
    km = km_ref[0]
    q = q_ref[0]
    gate = jnp.sum(km * q[None], axis=-1, keepdims=True)
    blk = _iota((nb, 1, 1), 0)
    rank = jnp.zeros(gate.shape, F32)
    for m in range(nb):
        gm = gate[m:m + 1]
        beats = (gm > gate) | ((gm == gate) & (blk > m))
        rank = rank + beats.astype(F32)
    lane = _iota((1, PAGE_IDS_W), 1)
    out = jnp.zeros((HEADS, PAGE_IDS_W), F32)
    for s in range(TOP_K):
        slot = (rank == s).astype(F32)
        for e, p_ref in enumerate((pe_ref, po_ref)):
            page = jnp.sum(slot * p_ref[0], axis=0)
            out = out + page * (lane == PAGES_PER_BLOCK * s + e).astype(F32)
    o_ref[0] = out.astype(I32)


def _sample_select(kmean_s, q_s, pages_even, pages_odd):
    bs, nb = kmean_s.shape[:2]
    return pl.pallas_call(
        functools.partial(_sample_select_kernel, nb=nb),
        out_shape=jax.ShapeDtypeStruct((bs, HEADS, PAGE_IDS_W), I32),
        grid=(bs,),
        in_specs=[
            pl.BlockSpec((1, nb, HEADS, HEAD_DIM), lambda bi: (bi, 0, 0, 0)),
            pl.BlockSpec((1, HEADS, HEAD_DIM), lambda bi: (bi, 0, 0)),
            pl.BlockSpec((1, nb, 1, 1), lambda bi: (bi, 0, 0, 0)),
            pl.BlockSpec((1, nb, 1, 1), lambda bi: (bi, 0, 0, 0)),
        ],
        out_specs=pl.BlockSpec((1, HEADS, PAGE_IDS_W), lambda bi: (bi, 0, 0)),
        compiler_params=_cparams(1, 32),
        name="sample_select",
    )(kmean_s, q_s, pages_even, pages_odd)


def _sample_attn_kernel(pages_ref, q_ref, kn_ref, vn_ref, ck_ref, cv_ref, o_ref, kbuf, vbuf, sem):
    bi = pl.program_id(0)
    slot = lax.rem(bi, 2)

    def copies(seq, buf):
        out = []
        for h in range(HEADS):
            for j in range(N_SEL_PAGES):
                page = pages_ref[(seq * HEADS + h) * PAGE_IDS_W + j]
                out.append(pltpu.make_async_copy(ck_ref.at[0, page, h], kbuf.at[buf, h, j], sem.at[0, buf]))
                out.append(pltpu.make_async_copy(cv_ref.at[0, page, h], vbuf.at[buf, h, j], sem.at[1, buf]))
        return out

    @pl.when(bi == 0)
    def _():
        for cp in copies(bi, slot):
            cp.start()

    @pl.when(bi + 1 < pl.num_programs(0))
    def _():
        for cp in copies(bi + 1, 1 - slot):
            cp.start()

    q_b = q_ref[0].astype(BF16)
    s_new = jnp.sum(q_b.astype(F32) * kn_ref[0].astype(BF16).astype(F32), axis=-1, keepdims=True) * ATT_SCALE
    v_new = vn_ref[0].astype(BF16).astype(F32)

    for cp in copies(bi, slot):
        cp.wait()

    hs, js = range(HEADS), range(N_SEL_PAGES)
    q8 = [jnp.broadcast_to(q_b[h:h + 1, :], (SUBLANES, HEAD_DIM)) for h in hs]
    s = [jnp.concatenate([_dot(q8[h], kbuf[slot, h, j].astype(BF16))[0:1, :] for j in js], axis=1) * ATT_SCALE
         for h in hs]
    m = [jnp.maximum(jnp.max(s[h], axis=-1, keepdims=True), s_new[h:h + 1, :]) for h in hs]
    p = [jnp.exp(s[h] - m[h]) for h in hs]
    pn = [jnp.exp(s_new[h:h + 1, :] - m[h]) for h in hs]
    denom = [jnp.sum(p[h], axis=-1, keepdims=True) + pn[h] for h in hs]
    p_b = [x.astype(BF16) for x in p]
    pv = [[_dot_nt(jnp.broadcast_to(p_b[h][:, j * PAGE_ROWS:(j + 1) * PAGE_ROWS], (SUBLANES, PAGE_ROWS)),
                   vbuf[slot, h, j].astype(BF16))[0:1, :] for j in js] for h in hs]
    for h in hs:
        acc = pn[h].astype(BF16).astype(F32) * v_new[h:h + 1, :]
        for j in js:
            acc = acc + pv[h][j]
        o_ref[0, h:h + 1, :] = acc / denom[h]


def _sample_attn(pages, q_s, k_new, v_new, cache_kt, cache_vt):
    bs = q_s.shape[0]
    vec = pl.BlockSpec((1, HEADS, HEAD_DIM), lambda bi, pg: (bi, 0, 0))
    return pl.pallas_call(
        _sample_attn_kernel,
        out_shape=jax.ShapeDtypeStruct((bs, HEADS, HEAD_DIM), F32),
        grid_spec=pltpu.PrefetchScalarGridSpec(
            num_scalar_prefetch=1,
            grid=(bs,),
            in_specs=[vec, vec, vec, pl.BlockSpec(memory_space=pl.ANY), pl.BlockSpec(memory_space=pl.ANY)],
            out_specs=vec,
            scratch_shapes=[
                pltpu.VMEM((2, HEADS, N_SEL_PAGES, HEAD_DIM, PAGE_ROWS), F32),
                pltpu.VMEM((2, HEADS, N_SEL_PAGES, HEAD_DIM, PAGE_ROWS), F32),
                pltpu.SemaphoreType.DMA((2, 2)),
            ],
        ),
        compiler_params=_cparams(1, 40),
        name="sample_attn",
    )(pages.reshape(-1), q_s, k_new, v_new, cache_kt, cache_vt)


def _sample_wkv_kernel(s_ref, dec_ref, kk_ref, b_ref, k_ref, v_ref, r_ref, so_ref, y_ref):
    st = s_ref[0]
    eye = (_iota((HEAD_DIM, HEAD_DIM), 0) == _iota((HEAD_DIM, HEAD_DIM), 1)).astype(F32)[None]
    sa = jnp.sum(st * kk_ref[0], axis=-1, keepdims=True)
    v_col = jnp.sum(v_ref[0] * eye, axis=-1, keepdims=True)
    s_new = st * dec_ref[0] - sa * b_ref[0] + v_col * k_ref[0]
    so_ref[0] = s_new
    y_col = jnp.sum(s_new * r_ref[0], axis=-1, keepdims=True)
    y_ref[0] = jnp.sum(y_col * eye, axis=1, keepdims=True)


def _sample_wkv(state, dec, kk, bvec, kmod, v, r):
    bs = state.shape[0]
    vec = pl.BlockSpec((1, HEADS, 1, HEAD_DIM), lambda bi: (bi, 0, 0, 0))
    st = pl.BlockSpec((1, HEADS, HEAD_DIM, HEAD_DIM), lambda bi: (bi, 0, 0, 0))
    as_vec = lambda x: x.reshape(bs, HEADS, 1, HEAD_DIM)
    return pl.pallas_call(
        _sample_wkv_kernel,
        out_shape=(
            jax.ShapeDtypeStruct(state.shape, F32),
            jax.ShapeDtypeStruct((bs, HEADS, 1, HEAD_DIM), F32),
        ),
        grid=(bs,),
        in_specs=[st] + [vec] * 6,
        out_specs=(st, vec),
        compiler_params=_cparams(1, 32),
        name="sample_wkv",
    )(state, as_vec(dec), as_vec(kk), as_vec(bvec), as_vec(kmod), as_vec(v), as_vec(r))


def _sample_rwkv_out_kernel(y_ref, r_ref, kmod_ref, v_ref, g_ref, rk_ref, lnw_ref, lnb_ref, o_ref):
    o_ref[...] = _rwkv_out(y_ref[...], r_ref[...], kmod_ref[...], v_ref[...], g_ref[...],
                           rk_ref[...], lnw_ref[...], lnb_ref[...], _head_ones())


def _sample_rwkv_out(y, r, kmod, v, g, r_k, ln_w, ln_b):
    return pl.pallas_call(
        _sample_rwkv_out_kernel,
        out_shape=jax.ShapeDtypeStruct(y.shape, F32),
        compiler_params=_cparams(0, 32),
        name="sample_rwkv_out",
    )(y, r, kmod, v, g, r_k, ln_w, ln_b)


def kernel(x_prompt, x_sample, cache_k, cache_v, state_wkv, state_shift, page_table, g_mix, w_in, g_q, g_k,
           mu_shift, w0, w_w2, a0, w_a2, w_g2, k_k, k_a, r_k, ln_x_w, ln_x_b, w_o, g_ffn, w_up, w_down):
    b, t, d = x_prompt.shape
    bs, ts, _ = x_sample.shape
    depth = w_in.shape[0]
    assert depth == 1 and ts == 1 and d == D_MODEL
    assert t % MOBA_BLOCK == 0 and t % STEP_ROWS == 0 and page_table.shape[1] % PAGES_PER_STEP == 0
    rwkv_w = 3 * D_MODEL
    off_rwkv, off_gate = 3 * D_MODEL, 6 * D_MODEL + LORA_W
    assert w_in.shape[2] == PROJ_W

    w = w_in[0]
    w_perm = jnp.concatenate(
        [w[:, :off_rwkv + rwkv_w], w[:, off_gate:], w[:, off_rwkv + rwkv_w:off_gate]], axis=1).astype(BF16)
    w_lora = jnp.zeros((LORA_W, 3 * D_MODEL), F32)
    w_lora = w_lora.at[:DECAY_LORA, :D_MODEL].set(w_w2[0])
    w_lora = w_lora.at[DECAY_LORA:DECAY_LORA + AAA_LORA, D_MODEL:2 * D_MODEL].set(w_a2[0])
    w_lora = w_lora.at[DECAY_LORA + AAA_LORA:, 2 * D_MODEL:].set(w_g2[0]).astype(BF16)
    vec = lambda x: x.reshape(1, -1).astype(F32)
    gq_t = vec(jnp.tile(g_q[0], HEADS))
    gk_t = vec(jnp.tile(g_k[0], HEADS))
    mu = mu_shift[0]
    mu_r, mu_k, mu_v, mu_l = (vec(mu[:D_MODEL]), vec(mu[D_MODEL:2 * D_MODEL]),
                              vec(mu[2 * D_MODEL:rwkv_w]), vec(mu[rwkv_w:]))
    rw = (vec(w0[0]), vec(a0[0]), w_lora, vec(k_k[0]), vec(k_a[0]))
    r_k_v, ln_w, ln_b = vec(r_k[0]), vec(ln_x_w[0]), vec(ln_x_b[0])
    g_mix_v, g_ffn_v = vec(g_mix[0]), vec(g_ffn[0])
    w_o_b, w_up_b, w_down_b = w_o[0].astype(BF16), w_up[0].astype(BF16), w_down[0].astype(BF16)

    def shift_rows(p_rows):
        return jnp.concatenate([p_rows[:, COL_R * D_MODEL:(COL_RV + 1) * D_MODEL],
                                p_rows[:, COL_LORA * LORA_W:]], axis=-1)

    xp = x_prompt.reshape(b * t, d)
    proj_p = _project(xp, g_mix_v, w_perm)
    k_p, v_p, q_pair, k_pair, vt, kmean = _attn_prep(proj_p, gq_t, gk_t, b, t)
    o_att = _moba_prompt(q_pair, k_pair, vt, kmean.reshape(b, t // MOBA_BLOCK, D_MODEL))
    o_rwkv, wkv_t = _rwkv_prompt(proj_p, mu_r, mu_k, mu_v, mu_l, *rw, r_k_v, ln_w, ln_b, b, t)
    y_p = _merge_ffn(xp, proj_p, o_att.reshape(b * t, d), o_rwkv, w_o_b, g_ffn_v, w_up_b, w_down_b)
    shift_p = shift_rows(proj_p.reshape(b, t, PROJ_W)[:, -1])

    xs = x_sample.reshape(bs, d)
    proj_s = _project(xs, g_mix_v, w_perm)
    sh = state_shift[0]
    (q_s, kn_s, r_s, dec_s, kmod_s, v_s, kk_s, b_s, g_s) = _sample_prep(
        proj_s, sh[:, :D_MODEL], sh[:, D_MODEL:2 * D_MODEL], sh[:, 2 * D_MODEL:rwkv_w], sh[:, rwkv_w:],
        gq_t, gk_t, mu_r, mu_k, mu_v, mu_l, *rw)
    v_new = proj_s[:, COL_V * D_MODEL:(COL_V + 1) * D_MODEL]
    heads = lambda x: x.reshape(bs, HEADS, HEAD_DIM)
    cache_kt = cache_k.transpose(0, 1, 3, 4, 2)
    cache_vt = cache_v.transpose(0, 1, 3, 4, 2)
    kmean_s = _page_block_means(cache_kt, page_table)
    n_blk = page_table.shape[1] // PAGES_PER_BLOCK
    pt_f = page_table.astype(F32).reshape(bs, n_blk, PAGES_PER_BLOCK)
    pages = _sample_select(kmean_s, heads(q_s), pt_f[:, :, 0].reshape(bs, n_blk, 1, 1),
                           pt_f[:, :, 1].reshape(bs, n_blk, 1, 1))
    o_att_s = _sample_attn(pages, heads(q_s), heads(kn_s), heads(v_new), cache_kt, cache_vt)
    wkv_s, y_s = _sample_wkv(state_wkv[0], dec_s, kk_s, b_s, kmod_s, v_s, r_s)
    o_rwkv_s = _sample_rwkv_out(y_s.reshape(bs, d), r_s, kmod_s, v_s, g_s, r_k_v, ln_w, ln_b)
    y_samp = _merge_ffn(xs, proj_s, o_att_s.reshape(bs, d), o_rwkv_s, w_o_b, g_ffn_v, w_up_b, w_down_b)

    hd = (HEADS, HEAD_DIM)
    return (
        y_p.reshape(b, t, d),
        y_samp.reshape(bs, ts, d),
        k_p.reshape(1, b, *hd, t).transpose(0, 1, 4, 2, 3),
        v_p.reshape(1, b, *hd, t).transpose(0, 1, 4, 2, 3),
        kn_s.reshape(1, bs, ts, *hd),
        v_new.reshape(1, bs, ts, *hd),
        wkv_t.reshape(1, b, HEADS, HEAD_DIM, HEAD_DIM),
        wkv_s[None],
        shift_p[None],
        shift_rows(proj_s)[None],
    )
```
